```python
import math
import jax
import jax.numpy as jnp
from jax import lax
import numpy as np

D_MODEL = 2048
BATCH = 2
SEQ = 16384
DEPTH = 2

GRID_W = 64
CTX_LEN = 256
ROPE_THETA = 10000.0
Q_BLOCK = 128
NORM_EPS = 1e-6
NEG_INF = -1e30

N_BRANCHES = 4
BRANCH_WIDTH = D_MODEL // 4
MLSTM_HEADS = 4
MLSTM_DV = BRANCH_WIDTH // MLSTM_HEADS
MLSTM_DK = MLSTM_DV // 2
MLSTM_CHUNK = 128
DIFF_HEADS = 4
DIFF_D = BRANCH_WIDTH // (2 * DIFF_HEADS)
SWA_HEADS = 8
SWA_KV_HEADS = 2
SWA_D = BRANCH_WIDTH // SWA_HEADS
SWA_WINDOW = 128
GQA_HEADS = 4
GQA_KV_HEADS = 2
GQA_D = BRANCH_WIDTH // GQA_HEADS
FFN_HIDDEN = -(-8 * D_MODEL // (3 * 256)) * 256

IN_SPLITS = (
    MLSTM_HEADS * MLSTM_DK,
    MLSTM_HEADS * MLSTM_DK,
    MLSTM_HEADS * MLSTM_DV,
    MLSTM_HEADS * MLSTM_DV,
    4 * MLSTM_HEADS,
    DIFF_HEADS * 2 * DIFF_D,
    DIFF_HEADS * 2 * DIFF_D,
    DIFF_HEADS * 2 * DIFF_D,
    SWA_HEADS * SWA_D,
    SWA_KV_HEADS * SWA_D,
    SWA_KV_HEADS * SWA_D,
    GQA_HEADS * GQA_D,
    GQA_KV_HEADS * GQA_D,
    GQA_KV_HEADS * GQA_D,
    N_BRANCHES * D_MODEL,
)
D_IN = sum(IN_SPLITS)

kernel_name = "hybrid_mlstm_diffattn_swa_gqa_dit_block"


def rms_norm(x, g):
    xf = x.astype(jnp.float32)
    y = xf * lax.rsqrt(jnp.mean(xf * xf, axis=-1, keepdims=True) + NORM_EPS)
    return (y * g.astype(jnp.float32)).astype(x.dtype)


def modulate(h, shift, scale):
    return h * (1 + scale) + shift


def split_cols(p):
    offsets = []
    acc = 0
    for s in IN_SPLITS[:-1]:
        acc += s
        offsets.append(acc)
    return jnp.split(p, offsets, axis=-1)


def axial_rope_tables(rows, head_dim):
    row = jnp.repeat(jnp.arange(rows, dtype=jnp.float32), GRID_W)
    col = jnp.tile(jnp.arange(GRID_W, dtype=jnp.float32), rows)
    n_freq = head_dim // 4
    inv = ROPE_THETA ** (-jnp.arange(n_freq, dtype=jnp.float32) / n_freq)
    ang = jnp.stack([row[:, None] * inv, col[:, None] * inv], axis=1)
    return jnp.cos(ang), jnp.sin(ang)


def apply_rope(x, cos, sin):
    b, n, h, d = x.shape
    xr = x.reshape(b, n, h, 2, 2, d // 4)
    x1, x2 = xr[..., 0, :], xr[..., 1, :]
    c = cos[None, :, None].astype(x.dtype)
    s = sin[None, :, None].astype(x.dtype)
    return jnp.stack([x1 * c - x2 * s, x2 * c + x1 * s], axis=-2).reshape(b, n, h, d)


def dense_gqa(q, k, v, sink=None):
    b, nq, hq, d = q.shape
    hkv, dv = k.shape[2], v.shape[-1]
    g = hq // hkv
    nb = nq // Q_BLOCK
    qb = q.reshape(b, nb, Q_BLOCK, hkv, g, d).transpose(1, 0, 2, 3, 4, 5)
    scale = d ** -0.5

    def block(qblk):
        s = jnp.einsum("bqhgd,bkhd->bhgqk", qblk, k).astype(jnp.float32) * scale
        if sink is not None:
            sk = jnp.broadcast_to(sink.astype(jnp.float32).reshape(1, hkv, g, 1, 1), s.shape[:-1] + (1,))
            p = jax.nn.softmax(jnp.concatenate([s, sk], axis=-1), axis=-1)[..., :-1]
        else:
            p = jax.nn.softmax(s, axis=-1)
        return jnp.einsum("bhgqk,bkhd->bqhgd", p.astype(v.dtype), v)

    o = lax.map(block, qb)
    return o.transpose(1, 0, 2, 3, 4, 5).reshape(b, nq, hq, dv)


def window_attention(q, k, v, k_ctx, v_ctx, sink):
    b, n, hq, d = q.shape
    hkv = k.shape[2]
    g = hq // hkv
    nb = n // Q_BLOCK
    nc = k_ctx.shape[1]
    span = Q_BLOCK + 2 * SWA_WINDOW
    pad = ((0, 0), (SWA_WINDOW, SWA_WINDOW), (0, 0), (0, 0))
    k_pad = jnp.pad(k, pad)
    v_pad = jnp.pad(v, pad)
    qb = q.reshape(b, nb, Q_BLOCK, hkv, g, d).transpose(1, 0, 2, 3, 4, 5)
    scale = d ** -0.5
    sink_l = sink.astype(jnp.float32).reshape(1, hkv, g, 1, 1)

    def block(args):
        j, qblk = args
        start = j * Q_BLOCK
        kb = lax.dynamic_slice_in_dim(k_pad, start, span, axis=1)
        vb = lax.dynamic_slice_in_dim(v_pad, start, span, axis=1)
        qpos = start + jnp.arange(Q_BLOCK)
        kpos = start - SWA_WINDOW + jnp.arange(span)
        valid = (jnp.abs(qpos[:, None] - kpos[None, :]) <= SWA_WINDOW) & (kpos[None, :] >= 0) & (kpos[None, :] < n)
        s_band = jnp.einsum("bqhgd,bkhd->bhgqk", qblk, kb).astype(jnp.float32) * scale
        s_band = jnp.where(valid, s_band, NEG_INF)
        s_ctx = jnp.einsum("bqhgd,bkhd->bhgqk", qblk, k_ctx).astype(jnp.float32) * scale
        s_sink = jnp.broadcast_to(sink_l, s_ctx.shape[:-1] + (1,))
        p = jax.nn.softmax(jnp.concatenate([s_ctx, s_band, s_sink], axis=-1), axis=-1).astype(v.dtype)
        return (jnp.einsum("bhgqk,bkhd->bqhgd", p[..., :nc], v_ctx)
                + jnp.einsum("bhgqk,bkhd->bqhgd", p[..., nc:nc + span], vb))

    o = lax.map(block, (jnp.arange(nb), qb))
    return o.transpose(1, 0, 2, 3, 4, 5).reshape(b, n, hq, v.shape[-1])


def diff_attention(q, k, v, lam):
    b, nq, h, _, d = q.shape
    nb = nq // Q_BLOCK
    qb = q.reshape(b, nb, Q_BLOCK, h, 2, d).transpose(1, 0, 2, 3, 4, 5)
    scale = d ** -0.5
    lam_f = lam.astype(jnp.float32)[None, :, None, None]

    def block(qblk):
        s = jnp.einsum("bqhmd,bkhmd->bhmqk", qblk, k).astype(jnp.float32) * scale
        p = jax.nn.softmax(s, axis=-1)
        a = p[:, :, 0] - lam_f * p[:, :, 1]
        return jnp.einsum("bhqk,bkhe->bqhe", a.astype(v.dtype), v)

    o = lax.map(block, qb)
    return o.transpose(1, 0, 2, 3, 4).reshape(b, nq, h, v.shape[-1])


def mlstm_chunked(q, k, v, ig, lf, state):
    b, n, h, dk = q.shape
    dv = v.shape[-1]
    L = MLSTM_CHUNK
    nc = n // L

    def to_chunks(a):
        a = a.astype(jnp.float32).reshape((b, nc, L, h) + a.shape[3:])
        return jnp.moveaxis(a, 2, 3).swapaxes(0, 1)

    xs = (to_chunks(q), to_chunks(k) * (dk ** -0.5), to_chunks(v), to_chunks(ig), to_chunks(lf))
    tril = jnp.tril(jnp.ones((L, L), dtype=bool))

    def step(carry, inp):
        C, nv, m = carry
        qc, kc, vc, ic, fc = inp
        bcum = jnp.cumsum(fc, axis=-1)
        dmat = bcum[..., :, None] - bcum[..., None, :] + ic[..., None, :]
        dmat = jnp.where(tril, dmat, NEG_INF)
        inter = bcum + m[..., None]
        m_t = jnp.maximum(inter, jnp.max(dmat, axis=-1))
        w_intra = jnp.exp(dmat - m_t[..., None])
        w_inter = jnp.exp(inter - m_t)
        s = jnp.einsum("bhtd,bhsd->bhts", qc, kc) * w_intra
        num = (w_inter[..., None] * jnp.einsum("bhvd,bhtd->bhtv", C, qc)
               + jnp.einsum("bhts,bhsv->bhtv", s, vc))
        den = w_inter * jnp.einsum("bhd,bhtd->bht", nv, qc) + jnp.sum(s, axis=-1)
        hout = num / jnp.maximum(jnp.abs(den), jnp.exp(-m_t))[..., None]
        b_last = bcum[..., -1]
        log_w = b_last[..., None] - bcum + ic
        m_new = jnp.maximum(b_last + m, jnp.max(log_w, axis=-1))
        decay = jnp.exp(b_last + m - m_new)
        w_s = jnp.exp(log_w - m_new[..., None])
        C_new = decay[..., None, None] * C + jnp.einsum("bhs,bhsv,bhsd->bhvd", w_s, vc, kc)
        n_new = decay[..., None] * nv + jnp.einsum("bhs,bhsd->bhd", w_s, kc)
        return (C_new, n_new, m_new), hout

    state, hs = lax.scan(step, state, xs)
    hs = jnp.moveaxis(hs.swapaxes(0, 1), 2, 3).reshape(b, n, h, dv)
    return hs.astype(v.dtype), state


def mlstm_mixer(pc, pl, gate_b, norm_g, ctx_out):
    def prep(p):
        q, k, v, o, gt = p
        b, n = q.shape[:2]
        q = q.reshape(b, n, MLSTM_HEADS, MLSTM_DK)
        k = k.reshape(b, n, MLSTM_HEADS, MLSTM_DK)
        v = v.reshape(b, n, MLSTM_HEADS, MLSTM_DV)
        gt = gt.astype(jnp.float32).reshape(b, n, 4, MLSTM_HEADS) + gate_b.astype(jnp.float32)
        fwd = (q, k, v, gt[:, :, 0], jax.nn.log_sigmoid(gt[:, :, 1]))
        bwd = tuple(jnp.flip(a, axis=1) for a in (q, k, v, gt[:, :, 2], jax.nn.log_sigmoid(gt[:, :, 3])))
        return fwd, bwd, o

    fc, bc, oc = prep(pc)
    fl, bl, ol = prep(pl)
    b = ol.shape[0]
    zero = (jnp.zeros((b, MLSTM_HEADS, MLSTM_DV, MLSTM_DK), jnp.float32),
            jnp.zeros((b, MLSTM_HEADS, MLSTM_DK), jnp.float32),
            jnp.zeros((b, MLSTM_HEADS), jnp.float32))
    hc_f, st_f = mlstm_chunked(*fc, zero)
    hl_f, _ = mlstm_chunked(*fl, st_f)
    hc_b, st_b = mlstm_chunked(*bc, zero)
    hl_b, _ = mlstm_chunked(*bl, st_b)
    g = norm_g.reshape(MLSTM_HEADS, MLSTM_DV)

    def finish(h, o):
        bb, n = h.shape[:2]
        h = rms_norm(h, g).reshape(bb, n, MLSTM_HEADS * MLSTM_DV)
        return h * jax.nn.sigmoid(o)

    yl = finish(hl_f + jnp.flip(hl_b, axis=1), ol)
    yc = finish(hc_f + jnp.flip(hc_b, axis=1), oc) if ctx_out else None
    return yc, yl


def diff_mixer(pc, pl, lam_params, norm_g, lam_init, rope, ctx_out):
    def heads(p):
        q, k, v = p
        b, n = q.shape[:2]
        return (q.reshape(b, n, 2 * DIFF_HEADS, DIFF_D),
                k.reshape(b, n, 2 * DIFF_HEADS, DIFF_D),
                v.reshape(b, n, DIFF_HEADS, 2 * DIFF_D))

    qc, kc, vc = heads(pc)
    ql, kl, vl = heads(pl)
    ql = apply_rope(ql, *rope)
    kl = apply_rope(kl, *rope)

    def pair(a):
        return a.reshape(a.shape[0], a.shape[1], DIFF_HEADS, 2, DIFF_D)

    lp = lam_params.astype(jnp.float32)
    lam = jnp.exp(jnp.sum(lp[0] * lp[1], axis=-1)) - jnp.exp(jnp.sum(lp[2] * lp[3], axis=-1)) + lam_init
    g = norm_g.reshape(DIFF_HEADS, 2 * DIFF_D)

    def finish(o):
        b, n = o.shape[:2]
        return (rms_norm(o, g) * (1.0 - lam_init)).reshape(b, n, DIFF_HEADS * 2 * DIFF_D)

    k_all = jnp.concatenate([pair(kc), pair(kl)], axis=1)
    v_all = jnp.concatenate([vc, vl], axis=1)
    yl = finish(diff_attention(pair(ql), k_all, v_all, lam))
    yc = finish(diff_attention(pair(qc), pair(kc), vc, lam)) if ctx_out else None
    return yc, yl


def swa_mixer(pc, pl, sink, rope, ctx_out):
    def heads(p):
        q, k, v = p
        b, n = q.shape[:2]
        return (q.reshape(b, n, SWA_HEADS, SWA_D),
                k.reshape(b, n, SWA_KV_HEADS, SWA_D),
                v.reshape(b, n, SWA_KV_HEADS, SWA_D))

    qc, kc, vc = heads(pc)
    ql, kl, vl = heads(pl)
    ql = apply_rope(ql, *rope)
    kl = apply_rope(kl, *rope)

    def flat(o):
        return o.reshape(o.shape[0], o.shape[1], SWA_HEADS * SWA_D)

    yl = flat(window_attention(ql, kl, vl, kc, vc, sink))
    yc = flat(dense_gqa(qc, kc, vc, sink)) if ctx_out else None
    return yc, yl


def gqa_mixer(pc, pl, q_norm_g, k_norm_g, rope, ctx_out):
    def heads(p):
        q, k, v = p
        b, n = q.shape[:2]
        q = rms_norm(q.reshape(b, n, GQA_HEADS, GQA_D), q_norm_g)
        k = rms_norm(k.reshape(b, n, GQA_KV_HEADS, GQA_D), k_norm_g)
        return q, k, v.reshape(b, n, GQA_KV_HEADS, GQA_D)

    qc, kc, vc = heads(pc)
    ql, kl, vl = heads(pl)
    ql = apply_rope(ql, *rope)
    kl = apply_rope(kl, *rope)

    def flat(o):
        return o.reshape(o.shape[0], o.shape[1], GQA_HEADS * GQA_D)

    yl = flat(dense_gqa(ql, jnp.concatenate([kc, kl], axis=1), jnp.concatenate([vc, vl], axis=1)))
    yc = flat(dense_gqa(qc, kc, vc)) if ctx_out else None
    return yc, yl


def merge_branches(ys, gate_pre, w_branch, w_out):
    b, n = gate_pre.shape[:2]
    gates = jax.nn.sigmoid(gate_pre.astype(jnp.float32)).astype(ys[0].dtype).reshape(b, n, N_BRANCHES, D_MODEL)
    acc = gates[:, :, 0] * (ys[0] @ w_branch[0])
    for i in range(1, N_BRANCHES):
        acc = acc + gates[:, :, i] * (ys[i] @ w_branch[i])
    return acc @ w_out


def hybrid_mixer(hc, hl, w_in, mlstm_gate_b, mlstm_norm_g, diff_lambda, diff_norm_g, lam_init,
                 swa_sink, gqa_q_norm_g, gqa_k_norm_g, w_branch, w_out, rope64, rope128, ctx_out):
    pc = split_cols(hc @ w_in)
    pl = split_cols(hl @ w_in)
    ya_c, ya_l = mlstm_mixer(pc[0:5], pl[0:5], mlstm_gate_b, mlstm_norm_g, ctx_out)
    yb_c, yb_l = diff_mixer(pc[5:8], pl[5:8], diff_lambda, diff_norm_g, lam_init, rope64, ctx_out)
    yc_c, yc_l = swa_mixer(pc[8:11], pl[8:11], swa_sink, rope64, ctx_out)
    yd_c, yd_l = gqa_mixer(pc[11:14], pl[11:14], gqa_q_norm_g, gqa_k_norm_g, rope128, ctx_out)
    out_l = merge_branches((ya_l, yb_l, yc_l, yd_l), pl[14], w_branch, w_out)
    out_c = merge_branches((ya_c, yb_c, yc_c, yd_c), pc[14], w_branch, w_out) if ctx_out else None
    return out_c, out_l


def swiglu(h, w_up, w_down):
    gate, up = jnp.split(h @ w_up, 2, axis=-1)
    return (jax.nn.silu(gate) * up) @ w_down


def setup_inputs(seed: int = 0) -> dict:
    key = jax.random.key(seed)
    ks = jax.random.split(key, 24)
    f32 = jnp.float32

    def nrm(k, shape, s):
        return s * jax.random.normal(k, shape, f32)

    zeros_h = jnp.zeros((MLSTM_HEADS,), f32)
    forget_base = jnp.linspace(3.0, 6.0, MLSTM_HEADS, dtype=f32)
    gate_base = jnp.stack([zeros_h, forget_base, zeros_h, forget_base])
    return {
        "x": nrm(ks[0], (BATCH, SEQ, D_MODEL), 1.0),
        "c": nrm(ks[1], (BATCH, D_MODEL), 1.0),
        "ctx": nrm(ks[2], (BATCH, CTX_LEN, D_MODEL), 1.0),
        "c_ctx": nrm(ks[3], (D_MODEL,), 1.0),
        "ada_w": nrm(ks[4], (DEPTH, D_MODEL, 6 * D_MODEL), 0.5 * D_MODEL ** -0.5),
        "ada_b": nrm(ks[5], (DEPTH, 6 * D_MODEL), 0.01),
        "norm1_g": 1.0 + nrm(ks[6], (DEPTH, D_MODEL), 0.01),
        "w_in": nrm(ks[7], (DEPTH, D_MODEL, D_IN), D_MODEL ** -0.5),
        "mlstm_gate_b": gate_base + nrm(ks[8], (DEPTH, 4, MLSTM_HEADS), 0.1),
        "mlstm_norm_g": 1.0 + nrm(ks[9], (DEPTH, MLSTM_HEADS * MLSTM_DV), 0.01),
        "diff_lambda": nrm(ks[10], (DEPTH, 4, DIFF_HEADS, DIFF_D), 0.1),
        "diff_norm_g": 1.0 + nrm(ks[11], (DEPTH, DIFF_HEADS * 2 * DIFF_D), 0.01),
        "swa_sink": nrm(ks[12], (DEPTH, SWA_HEADS), 0.5),
        "gqa_q_norm_g": 1.0 + nrm(ks[13], (DEPTH, GQA_D), 0.01),
        "gqa_k_norm_g": 1.0 + nrm(ks[14], (DEPTH, GQA_D), 0.01),
        "w_branch": nrm(ks[15], (DEPTH, N_BRANCHES, BRANCH_WIDTH, D_MODEL), BRANCH_WIDTH ** -0.5),
        "w_out": nrm(ks[16], (DEPTH, D_MODEL, D_MODEL), D_MODEL ** -0.5),
        "norm2_g": 1.0 + nrm(ks[17], (DEPTH, D_MODEL), 0.01),
        "w_up": nrm(ks[18], (DEPTH, D_MODEL, 2 * FFN_HIDDEN), D_MODEL ** -0.5),
        "w_down": nrm(ks[19], (DEPTH, FFN_HIDDEN, D_MODEL), FFN_HIDDEN ** -0.5),
        "final_norm_g": 1.0 + nrm(ks[20], (D_MODEL,), 0.01),
    }


def reference(x, c, ctx, c_ctx, ada_w, ada_b, norm1_g, w_in, mlstm_gate_b, mlstm_norm_g,
              diff_lambda, diff_norm_g, swa_sink, gqa_q_norm_g, gqa_k_norm_g, w_branch, w_out,
              norm2_g, w_up, w_down, final_norm_g):
    n_lat = x.shape[1]
    rows = n_lat // GRID_W
    rope64 = axial_rope_tables(rows, DIFF_D)
    rope128 = axial_rope_tables(rows, GQA_D)
    s_lat = jax.nn.silu(c.astype(jnp.float32))
    s_ctx = jax.nn.silu(c_ctx.astype(jnp.float32))
    xc = ctx
    for l in range(DEPTH):
        last = l == DEPTH - 1
        mod_l = (s_lat @ ada_w[l] + ada_b[l]).astype(x.dtype)[:, None, :]
        mod_c = (s_ctx @ ada_w[l] + ada_b[l]).astype(x.dtype)[None, None, :]
        sh1_l, sc1_l, g1_l, sh2_l, sc2_l, g2_l = jnp.split(mod_l, 6, axis=-1)
        sh1_c, sc1_c, g1_c, sh2_c, sc2_c, g2_c = jnp.split(mod_c, 6, axis=-1)
        lam_init = 0.8 - 0.6 * math.exp(-0.3 * l)

        hl = modulate(rms_norm(x, norm1_g[l]), sh1_l, sc1_l)
        hc = modulate(rms_norm(xc, norm1_g[l]), sh1_c, sc1_c)
        out_c, out_l = hybrid_mixer(hc, hl, w_in[l], mlstm_gate_b[l], mlstm_norm_g[l], diff_lambda[l],
                                    diff_norm_g[l], lam_init, swa_sink[l], gqa_q_norm_g[l], gqa_k_norm_g[l],
                                    w_branch[l], w_out[l], rope64, rope128, not last)
        x = x + g1_l * out_l
        hl = modulate(rms_norm(x, norm2_g[l]), sh2_l, sc2_l)
        x = x + g2_l * swiglu(hl, w_up[l], w_down[l])
        if not last:
            xc = xc + g1_c * out_c
            hc = modulate(rms_norm(xc, norm2_g[l]), sh2_c, sc2_c)
            xc = xc + g2_c * swiglu(hc, w_up[l], w_down[l])
    return rms_norm(x, final_norm_g)
```

```python
import functools
import math

import jax
import jax.numpy as jnp
from jax import lax
from jax.experimental import pallas as pl
from jax.experimental.pallas import tpu as pltpu

F32 = jnp.float32
BF16 = jnp.bfloat16

D_MODEL = 2048
GRID_W = 64
ROPE_THETA = 10000.0
NORM_EPS = 1e-6
NEG_INF = -1e30
LOG2E = 1.4426950408889634

N_BRANCHES = 4
BRANCH_WIDTH = 512
MLSTM_HEADS, MLSTM_DK, MLSTM_DV, MLSTM_CHUNK = 4, 64, 128, 128
DIFF_HEADS, DIFF_D = 4, 64
SWA_HEADS, SWA_KV_HEADS, SWA_D, SWA_WINDOW = 8, 2, 64, 128
GQA_HEADS, GQA_KV_HEADS, GQA_D = 4, 2, 128
FFN_HIDDEN = 5632

LANES = 128
VMEM_LIMIT_BYTES = 56 * 1024 * 1024

_SRC = {}
_acc = 0
for _name, _w in (("mq", 256), ("mk", 256), ("mv", 512), ("mo", 512), ("mg", 16),
                  ("dq", 512), ("dk", 512), ("dv", 512), ("sq", 512), ("sk", 128), ("sv", 128),
                  ("gq", 512), ("gk", 256), ("gv", 256), ("gates", 8192)):
    _SRC[_name] = (_acc, _w)
    _acc += _w

_P_ORDER = ("dq", "dk", "sq", "sk", "gq", "gk", "sv", "mv", "mo", "mq", "mk", "gates", "dv", "gv")
P_OFF = {}
_acc = 0
for _name in _P_ORDER:
    P_OFF[_name] = _acc
    _acc += _SRC[_name][1]
P_WIDTH = _acc
QK_WIDTH = P_OFF["sv"]
PROJ_TN = 768


def _cparams(sem):
    return pltpu.CompilerParams(dimension_semantics=sem, vmem_limit_bytes=VMEM_LIMIT_BYTES)


def _dot(a, b):
    return jnp.dot(a, b, preferred_element_type=F32)


def _dot_nt(a, b):
    return lax.dot_general(a, b, (((1,), (1,)), ((), ())), preferred_element_type=F32)


def _ada_kernel(s_ref, w_ref, b_ref, o_ref):
    s = s_ref[...]
    s = s * jax.nn.sigmoid(s)
    o_ref[0] = _dot(s.astype(BF16), w_ref[0].astype(BF16)) + b_ref[0]


def _ada_modulation(cond_rows, ada_w, ada_b):
    depth, d, n6 = ada_w.shape
    rows = cond_rows.shape[0]
    tn = 1024
    return pl.pallas_call(
        _ada_kernel,
        out_shape=jax.ShapeDtypeStruct((depth, rows, n6), F32),
        grid=(depth, n6 // tn),
        in_specs=[
            pl.BlockSpec((rows, d), lambda l, j: (0, 0)),
            pl.BlockSpec((1, d, tn), lambda l, j: (l, 0, j)),
            pl.BlockSpec((1, 1, tn), lambda l, j: (l, 0, j)),
        ],
        out_specs=pl.BlockSpec((1, rows, tn), lambda l, j: (l, 0, j)),
        compiler_params=_cparams(("arbitrary", "arbitrary")),
        name="ada_modulation",
    )(cond_rows, ada_w, ada_b.reshape(depth, 1, n6))


def _norm_mod(x, g, shift, scale):
    y = x * lax.rsqrt(jnp.mean(x * x, axis=-1, keepdims=True) + NORM_EPS)
    return (y * g) * (1.0 + scale) + shift


def _proj_kernel(x_ref, g_ref, sh_ref, sc_ref, w_ref, wmg_ref, p_ref, mg_ref, h_scr):
    @pl.when(pl.program_id(2) == 0)
    def _():
        h = _norm_mod(x_ref[0], g_ref[...], sh_ref[0], sc_ref[0]).astype(BF16)
        h_scr[...] = h
        mg_ref[0] = _dot(h, wmg_ref[...])

    p_ref[0] = _dot(h_scr[...], w_ref[...]).astype(BF16)


def _input_projection(x, g, mod, w_p, w_mg, tm):
    b, n, d = x.shape
    bm = mod.shape[0]
    mod_b = (lambda bi: bi) if bm == b else (lambda bi: 0)
    return pl.pallas_call(
        _proj_kernel,
        out_shape=(jax.ShapeDtypeStruct((b, n, P_WIDTH), BF16),
                   jax.ShapeDtypeStruct((b, n, 2 * LANES), F32)),
        grid=(b, n // tm, P_WIDTH // PROJ_TN),
        in_specs=[
            pl.BlockSpec((1, tm, d), lambda bi, i, j: (bi, i, 0)),
            pl.BlockSpec((1, d), lambda bi, i, j: (0, 0)),
            pl.BlockSpec((1, 1, d), lambda bi, i, j: (mod_b(bi), 0, 0)),
            pl.BlockSpec((1, 1, d), lambda bi, i, j: (mod_b(bi), 0, 1)),
            pl.BlockSpec((d, PROJ_TN), lambda bi, i, j: (0, j)),
            pl.BlockSpec((d, 2 * LANES), lambda bi, i, j: (0, 0)),
        ],
        out_specs=(pl.BlockSpec((1, tm, PROJ_TN), lambda bi, i, j: (bi, i, j)),
                   pl.BlockSpec((1, tm, 2 * LANES), lambda bi, i, j: (bi, i, 0))),
        scratch_shapes=[pltpu.VMEM((tm, d), BF16)],
        compiler_params=_cparams(("arbitrary", "arbitrary", "arbitrary")),
        name="input_projection",
    )(x, g.reshape(1, d), mod, mod, w_p, w_mg)


def _rope(x, cos, sin_lo, sin_hi, half):
    width = x.shape[-1]
    return (x * cos + pltpu.roll(x, width - half, 1) * sin_lo + pltpu.roll(x, half, 1) * sin_hi)


def _prep_kernel(p_ref, t64_ref, t128_ref, qg_ref, kg_ref, o_ref):
    x = p_ref[0].astype(F32)
    c64, sl64, sh64 = t64_ref[0], t64_ref[1], t64_ref[2]
    c128, sl128, sh128 = t128_ref[0], t128_ref[1], t128_ref[2]
    qscale64 = LOG2E * DIFF_D ** -0.5
    qscale128 = LOG2E * GQA_D ** -0.5
    for blk in range(QK_WIDTH // LANES):
        lo = blk * LANES
        xb = x[:, lo:lo + LANES]
        if lo < P_OFF["gq"]:
            is_q = (lo < P_OFF["dk"]) or (P_OFF["sq"] <= lo < P_OFF["sk"])
            y = _rope(xb, c64, sl64, sh64, DIFF_D // 4)
            if is_q:
                y = y * qscale64
        else:
            is_q = lo < P_OFF["gk"]
            g = qg_ref[...] if is_q else kg_ref[...]
            y = xb * lax.rsqrt(jnp.mean(xb * xb, axis=-1, keepdims=True) + NORM_EPS) * g
            y = _rope(y, c128, sl128, sh128, GQA_D // 4)
            if is_q:
                y = y * qscale128
        o_ref[0, :, lo:lo + LANES] = y.astype(BF16)


def _prep_qk(p, t64, t128, qg, kg, tm):
    b, n, _ = p.shape
    return pl.pallas_call(
        _prep_kernel,
        out_shape=jax.ShapeDtypeStruct((b, n, QK_WIDTH), BF16),
        grid=(b, n // tm),
        in_specs=[
            pl.BlockSpec((1, tm, QK_WIDTH), lambda bi, i: (bi, i, 0)),
            pl.BlockSpec((3, tm, LANES), lambda bi, i: (0, i, 0)),
            pl.BlockSpec((3, tm, LANES), lambda bi, i: (0, i, 0)),
            pl.BlockSpec((1, LANES), lambda bi, i: (0, 0)),
            pl.BlockSpec((1, LANES), lambda bi, i: (0, 0)),
        ],
        out_specs=pl.BlockSpec((1, tm, QK_WIDTH), lambda bi, i: (bi, i, 0)),
        compiler_params=_cparams(("arbitrary", "arbitrary")),
        name="prep_qk",
    )(p, t64, t128, qg.reshape(1, LANES), kg.reshape(1, LANES))


def _rope_tables(n, head_dim, identity):
    n_freq = head_dim // 4
    lane = jnp.arange(LANES)
    first_half = (lane % (2 * n_freq)) < n_freq
    if identity:
        cos = jnp.ones((n, LANES), F32)
        zero = jnp.zeros((n, LANES), F32)
        return jnp.stack([cos, zero, zero])
    t = jnp.arange(n)
    pos = jnp.stack([(t // GRID_W).astype(F32), (t % GRID_W).astype(F32)], axis=1)
    inv = ROPE_THETA ** (-jnp.arange(n_freq, dtype=F32) / n_freq)
    ang = pos[:, :, None] * inv
    axis = (lane % head_dim) // (2 * n_freq)
    freq = lane % n_freq
    ang_l = ang[:, axis, freq]
    cos, sin = jnp.cos(ang_l), jnp.sin(ang_l)
    sin_lo = jnp.where(first_half, -sin, 0.0)
    sin_hi = jnp.where(first_half, 0.0, sin)
    return jnp.stack([cos, sin_lo, sin_hi])


def _log_sigmoid(x):
    return jnp.minimum(x, 0.0) - jnp.log(1.0 + jnp.exp(-jnp.abs(x)))


def _split_dot(tri, x):
    hi = x.astype(BF16)
    r1 = x - hi.astype(F32)
    mid = r1.astype(BF16)
    lo = (r1 - mid.astype(F32)).astype(BF16)
    return _dot(tri, hi) + _dot(tri, mid) + _dot(tri, lo)


def _mlstm_kernel(*refs, reverse, finish):
    if finish:
        (q_ref, k_ref, v_ref, mg_ref, gb_ref, c0_ref, m0_ref, hf_ref, o_ref, ng_ref,
         out_ref, cfin_ref, mfin_ref, c_scr, m_scr) = refs
    else:
        (q_ref, k_ref, v_ref, mg_ref, gb_ref, c0_ref, m0_ref,
         out_ref, cfin_ref, mfin_ref, c_scr, m_scr) = refs
    L = MLSTM_CHUNK
    j = pl.program_id(1)

    @pl.when(j == 0)
    def _():
        c_scr[...] = c0_ref[0]
        m_scr[...] = m0_ref[0]

    q_all = q_ref[0]
    k_all = k_ref[0]
    v_all = v_ref[0]
    kt_all = k_all.astype(F32).T.astype(BF16)

    gt = mg_ref[0] + gb_ref[...]
    ig = gt[:, :LANES]
    lf = _log_sigmoid(gt[:, LANES:])
    row = lax.broadcasted_iota(jnp.int32, (L, L), 0)
    col = lax.broadcasted_iota(jnp.int32, (L, L), 1)
    mask = (col >= row) if reverse else (col <= row)
    tri = jnp.where(mask, 1.0, 0.0).astype(BF16)
    bc = _split_dot(tri, lf)
    a = ig - bc
    a_t = a.T
    total = bc[0:1, :] if reverse else bc[L - 1:L, :]
    ones_col = jnp.where(col == 0, 1.0, 0.0).astype(BF16)

    dir_base = MLSTM_HEADS if reverse else 0
    houts = []
    for h in range(MLSTM_HEADS):
        c = dir_base + h
        a_row = a_t[c:c + 1, :]
        a_col = a[:, c:c + 1]
        bc_col = bc[:, c:c + 1]
        tot = total[:, c:c + 1]
        m_prev = m_scr[h][:, 0:1]
        dm = jnp.where(mask, jnp.broadcast_to(a_row, (L, L)), NEG_INF)
        cm = jnp.max(dm, axis=1, keepdims=True)
        mt = jnp.maximum(cm, m_prev)
        w_intra = jnp.exp(dm - mt)
        w_inter = jnp.exp(m_prev - mt)
        q = q_all[:, h * MLSTM_DK:(h + 1) * MLSTM_DK]
        k = k_all[:, h * MLSTM_DK:(h + 1) * MLSTM_DK]
        v = v_all[:, h * MLSTM_DV:(h + 1) * MLSTM_DV]
        v_ext = jnp.concatenate([v, ones_col], axis=1)
        kscale = MLSTM_DK ** -0.5
        s = _dot_nt(q, k) * kscale * w_intra
        r = _dot(s.astype(BF16), v_ext)
        qc = _dot(q, c_scr[h].astype(BF16))
        t = w_inter * qc + r
        num = t[:, :MLSTM_DV]
        den = t[:, MLSTM_DV:MLSTM_DV + 1]
        m_t = bc_col + mt
        houts.append(num / jnp.maximum(jnp.abs(den), jnp.exp(-m_t)))
        a_max = jnp.max(a_row, axis=1, keepdims=True)
        m_new = tot + jnp.maximum(m_prev, a_max)
        decay = jnp.exp(tot + m_prev - m_new)
        w_s = jnp.exp(tot + a_col - m_new)
        kt = kt_all[h * MLSTM_DK:(h + 1) * MLSTM_DK, :]
        upd = _dot(kt, (w_s * v_ext.astype(F32)).astype(BF16))
        c_scr[h] = decay * c_scr[h] + kscale * upd
        m_scr[h] = jnp.broadcast_to(m_new, (1, LANES))

    hcat = jnp.concatenate(houts, axis=1)
    if finish:
        hsum = hf_ref[0] + hcat
        ys = []
        for h in range(MLSTM_HEADS):
            hh = hsum[:, h * MLSTM_DV:(h + 1) * MLSTM_DV]
            ys.append(hh * lax.rsqrt(jnp.mean(hh * hh, axis=-1, keepdims=True) + NORM_EPS))
        y = jnp.concatenate(ys, axis=1) * ng_ref[...]
        out_ref[0] = (y * jax.nn.sigmoid(o_ref[0].astype(F32))).astype(out_ref.dtype)
    else:
        out_ref[0] = hcat

    @pl.when(j == pl.num_programs(1) - 1)
    def _():
        cfin_ref[0] = c_scr[...]
        mfin_ref[0] = m_scr[...]


def _mlstm_scan(p, mg, gate_b, c0, m0, hf, norm_g, *, reverse):
    b, n, _ = p.shape
    L = MLSTM_CHUNK
    nc = n // L
    finish = hf is not None
    cidx = (lambda j: nc - 1 - j) if reverse else (lambda j: j)
    w256, w512 = 256, 512
    in_specs = [
        pl.BlockSpec((1, L, w256), lambda bi, j: (bi, cidx(j), P_OFF["mq"] // w256)),
        pl.BlockSpec((1, L, w256), lambda bi, j: (bi, cidx(j), P_OFF["mk"] // w256)),
        pl.BlockSpec((1, L, w512), lambda bi, j: (bi, cidx(j), P_OFF["mv"] // w512)),
        pl.BlockSpec((1, L, 2 * LANES), lambda bi, j: (bi, cidx(j), 0)),
        pl.BlockSpec((1, 2 * LANES), lambda bi, j: (0, 0)),
        pl.BlockSpec((1, MLSTM_HEADS, MLSTM_DK, 2 * LANES), lambda bi, j: (bi, 0, 0, 0)),
        pl.BlockSpec((1, MLSTM_HEADS, 1, LANES), lambda bi, j: (bi, 0, 0, 0)),
    ]
    args = [p, p, p, mg, gate_b, c0, m0]
    if finish:
        in_specs += [
            pl.BlockSpec((1, L, w512), lambda bi, j: (bi, cidx(j), 0)),
            pl.BlockSpec((1, L, w512), lambda bi, j: (bi, cidx(j), P_OFF["mo"] // w512)),
            pl.BlockSpec((1, w512), lambda bi, j: (0, 0)),
        ]
        args += [hf, p, norm_g.reshape(1, w512)]
    return pl.pallas_call(
        functools.partial(_mlstm_kernel, reverse=reverse, finish=finish),
        out_shape=(jax.ShapeDtypeStruct((b, n, w512), BF16 if finish else F32),
                   jax.ShapeDtypeStruct(c0.shape, F32),
                   jax.ShapeDtypeStruct(m0.shape, F32)),
        grid=(b, nc),
        in_specs=in_specs,
        out_specs=(pl.BlockSpec((1, L, w512), lambda bi, j: (bi, cidx(j), 0)),
                   pl.BlockSpec((1, MLSTM_HEADS, MLSTM_DK, 2 * LANES), lambda bi, j: (bi, 0, 0, 0)),
                   pl.BlockSpec((1, MLSTM_HEADS, 1, LANES), lambda bi, j: (bi, 0, 0, 0))),
        scratch_shapes=[pltpu.VMEM((MLSTM_HEADS, MLSTM_DK, 2 * LANES), F32),
                        pltpu.VMEM((MLSTM_HEADS, 1, LANES), F32)],
        compiler_params=_cparams(("arbitrary", "arbitrary")),
        name="mlstm_bwd" if reverse else "mlstm_fwd",
    )(*args)


def _mlstm_mixer(p_c, mg_c, p_l, mg_l, gate_b, norm_g, ctx_out):
    b = p_l.shape[0]
    c0 = jnp.zeros((b, MLSTM_HEADS, MLSTM_DK, 2 * LANES), F32)
    m0 = jnp.zeros((b, MLSTM_HEADS, 1, LANES), F32)
    hc_f, cf, mf = _mlstm_scan(p_c, mg_c, gate_b, c0, m0, None, None, reverse=False)
    hl_f, _, _ = _mlstm_scan(p_l, mg_l, gate_b, cf, mf, None, None, reverse=False)
    yc, cb, mb = _mlstm_scan(p_c, mg_c, gate_b, c0, m0, hc_f, norm_g, reverse=True)
    yl, _, _ = _mlstm_scan(p_l, mg_l, gate_b, cb, mb, hl_f, norm_g, reverse=True)
    return (yc if ctx_out else None), yl


def _ones_lane0(rows):
    col = lax.broadcasted_iota(jnp.int32, (rows, LANES), 1)
    return jnp.where(col == 0, 1.0, 0.0).astype(BF16)


def _online_softmax_step(q, k, v_ext, m_ref, acc_ref, idx):
    s = _dot_nt(q, k)
    m_prev = m_ref[idx]
    m_new = jnp.maximum(m_prev, jnp.max(s, axis=1, keepdims=True))
    alpha = jnp.exp2(m_prev - m_new)
    p = jnp.exp2(s - m_new[:, 0:1])
    pv = _dot(p.astype(BF16), v_ext)
    acc_ref[idx] = jnp.concatenate([alpha, alpha], axis=1) * acc_ref[idx] + pv
    m_ref[idx] = m_new


def _diff_kernel(*refs, has_ctx, lam_init):
    if has_ctx:
        (q_ref, k_ref, v_ref, kc_ref, vc_ref, lam_ref, g_ref, o_ref, m_scr, acc_scr) = refs
    else:
        (q_ref, k_ref, v_ref, lam_ref, g_ref, o_ref, m_scr, acc_scr) = refs
    kj = pl.program_id(3)
    d = DIFF_D

    @pl.when(kj == 0)
    def _():
        m_scr[...] = jnp.full(m_scr.shape, NEG_INF, F32)
        acc_scr[...] = jnp.zeros(acc_scr.shape, F32)

    q = q_ref[0]

    def attend(k, v):
        v_ext = jnp.concatenate([v, _ones_lane0(v.shape[0])], axis=1)
        for sub in range(2):
            _online_softmax_step(q[:, sub * d:(sub + 1) * d], k[:, sub * d:(sub + 1) * d],
                                 v_ext, m_scr, acc_scr, sub)

    if has_ctx:
        @pl.when(kj == 0)
        def _():
            attend(kc_ref[0], vc_ref[0])

    attend(k_ref[0], v_ref[0])

    @pl.when(kj == pl.num_programs(3) - 1)
    def _():
        lp = lam_ref[0]
        lam = (jnp.exp(jnp.sum(lp[0:1] * lp[1:2], axis=1, keepdims=True))
               - jnp.exp(jnp.sum(lp[2:3] * lp[3:4], axis=1, keepdims=True)) + lam_init)
        a1, a2 = acc_scr[0], acc_scr[1]
        o1 = a1[:, :LANES] / a1[:, LANES:LANES + 1]
        o2 = a2[:, :LANES] / a2[:, LANES:LANES + 1]
        o = o1 - lam * o2
        y = o * lax.rsqrt(jnp.mean(o * o, axis=-1, keepdims=True) + NORM_EPS) * g_ref[...]
        o_ref[0] = (y * (1.0 - lam_init)).astype(o_ref.dtype)


def _diff_attention(qk_q, qk_k, p_v, qk_c, p_c, lam_params, norm_g, lam_init, tq, tk):
    b, nq, _ = qk_q.shape
    nk = qk_k.shape[1]
    has_ctx = qk_c is not None
    kcol = P_OFF["dk"] // LANES
    vcol = P_OFF["dv"] // LANES
    in_specs = [
        pl.BlockSpec((1, tq, LANES), lambda bi, h, i, j: (bi, i, h)),
        pl.BlockSpec((1, tk, LANES), lambda bi, h, i, j: (bi, j, kcol + h)),
        pl.BlockSpec((1, tk, LANES), lambda bi, h, i, j: (bi, j, vcol + h)),
    ]
    args = [qk_q, qk_k, p_v]
    if has_ctx:
        nc = qk_c.shape[1]
        in_specs += [
            pl.BlockSpec((1, nc, LANES), lambda bi, h, i, j: (bi, 0, kcol + h)),
            pl.BlockSpec((1, nc, LANES), lambda bi, h, i, j: (bi, 0, vcol + h)),
        ]
        args += [qk_c, p_c]
    in_specs += [
        pl.BlockSpec((1, 4, DIFF_D), lambda bi, h, i, j: (h, 0, 0)),
        pl.BlockSpec((1, LANES), lambda bi, h, i, j: (0, h)),
    ]
    args += [jnp.transpose(lam_params, (1, 0, 2)), norm_g.reshape(1, DIFF_HEADS * LANES)]
    return pl.pallas_call(
        functools.partial(_diff_kernel, has_ctx=has_ctx, lam_init=lam_init),
        out_shape=jax.ShapeDtypeStruct((b, nq, DIFF_HEADS * LANES), BF16),
        grid=(b, DIFF_HEADS, nq // tq, nk // tk),
        in_specs=in_specs,
        out_specs=pl.BlockSpec((1, tq, LANES), lambda bi, h, i, j: (bi, i, h)),
        scratch_shapes=[pltpu.VMEM((2, tq, LANES), F32), pltpu.VMEM((2, tq, 2 * LANES), F32)],
        compiler_params=_cparams(("arbitrary",) * 4),
        name="diff_attention",
    )(*args)


def _gqa_kernel(*refs, has_ctx):
    if has_ctx:
        (qa_ref, qb_ref, k_ref, v_ref, kc_ref, vc_ref, o_ref, q_scr, m_scr, acc_scr) = refs
    else:
        (qa_ref, qb_ref, k_ref, v_ref, o_ref, q_scr, m_scr, acc_scr) = refs
    kj = pl.program_id(3)
    tq = qa_ref.shape[1]

    @pl.when(kj == 0)
    def _():
        q_scr[0:tq, :] = qa_ref[0]
        q_scr[tq:2 * tq, :] = qb_ref[0]
        m_scr[...] = jnp.full(m_scr.shape, NEG_INF, F32)
        acc_scr[...] = jnp.zeros(acc_scr.shape, F32)

    def attend(k, v):
        v_ext = jnp.concatenate([v, _ones_lane0(v.shape[0])], axis=1)
        _online_softmax_step(q_scr[...], k, v_ext, m_scr, acc_scr, 0)

    if has_ctx:
        @pl.when(kj == 0)
        def _():
            attend(kc_ref[0], vc_ref[0])

    attend(k_ref[0], v_ref[0])

    @pl.when(kj == pl.num_programs(3) - 1)
    def _():
        a = acc_scr[0]
        o = a[:, :LANES] / a[:, LANES:LANES + 1]
        o_ref[0] = jnp.concatenate([o[0:tq], o[tq:2 * tq]], axis=1).astype(o_ref.dtype)


def _gqa_attention(qk_q, qk_k, p_v, qk_c, p_c, tq, tk):
    b, nq, _ = qk_q.shape
    nk = qk_k.shape[1]
    has_ctx = qk_c is not None
    qcol = P_OFF["gq"] // LANES
    kcol = P_OFF["gk"] // LANES
    vcol = P_OFF["gv"] // LANES
    in_specs = [
        pl.BlockSpec((1, tq, LANES), lambda bi, h, i, j: (bi, i, qcol + 2 * h)),
        pl.BlockSpec((1, tq, LANES), lambda bi, h, i, j: (bi, i, qcol + 2 * h + 1)),
        pl.BlockSpec((1, tk, LANES), lambda bi, h, i, j: (bi, j, kcol + h)),
        pl.BlockSpec((1, tk, LANES), lambda bi, h, i, j: (bi, j, vcol + h)),
    ]
    args = [qk_q, qk_q, qk_k, p_v]
    if has_ctx:
        nc = qk_c.shape[1]
        in_specs += [
            pl.BlockSpec((1, nc, LANES), lambda bi, h, i, j: (bi, 0, kcol + h)),
            pl.BlockSpec((1, nc, LANES), lambda bi, h, i, j: (bi, 0, vcol + h)),
        ]
        args += [qk_c, p_c]
    return pl.pallas_call(
        functools.partial(_gqa_kernel, has_ctx=has_ctx),
        out_shape=jax.ShapeDtypeStruct((b, nq, GQA_HEADS * GQA_D), BF16),
        grid=(b, GQA_KV_HEADS, nq // tq, nk // tk),
        in_specs=in_specs,
        out_specs=pl.BlockSpec((1, tq, 2 * LANES), lambda bi, h, i, j: (bi, i, h)),
        scratch_shapes=[pltpu.VMEM((2 * tq, LANES), BF16),
                        pltpu.VMEM((1, 2 * tq, LANES), F32),
                        pltpu.VMEM((1, 2 * tq, 2 * LANES), F32)],
        compiler_params=_cparams(("arbitrary",) * 4),
        name="gqa_attention",
    )(*args)


def _swa_kernel(*refs, band, n_lat):
    if band:
        (q_ref, kp_ref, kc_ref, kn_ref, vp_ref, vc_ref, vn_ref, kx_ref, vx_ref, sink_ref,
         o_ref) = refs
    else:
        (q_ref, kx_ref, vx_ref, sink_ref, o_ref) = refs
    tq = q_ref.shape[1]
    d = SWA_D
    g = SWA_HEADS // SWA_KV_HEADS
    q = q_ref[0]
    kx = kx_ref[0]
    vx = vx_ref[0]
    nx = kx.shape[0]
    if band:
        w = SWA_WINDOW
        i = pl.program_id(1)
        k_all = jnp.concatenate([kx, kp_ref[0], kc_ref[0], kn_ref[0]], axis=0)
        v_all = jnp.concatenate([vx, vp_ref[0], vc_ref[0], vn_ref[0]], axis=0)
        nkeys = nx + tq + 2 * w
        qpos = i * tq + lax.broadcasted_iota(jnp.int32, (tq, nkeys), 0)
        kidx = lax.broadcasted_iota(jnp.int32, (tq, nkeys), 1)
        kpos = i * tq - w + (kidx - nx)
        valid = (kidx < nx) | ((jnp.abs(qpos - kpos) <= w) & (kpos >= 0) & (kpos < n_lat))
        bias = jnp.where(valid, 0.0, NEG_INF)
        bias = jnp.concatenate([bias] * g, axis=0)
    else:
        k_all, v_all = kx, vx
        bias = None
    sink = sink_ref[...] * LOG2E
    outs = []
    for kvh in range(SWA_KV_HEADS):
        qs = jnp.concatenate(
            [q[:, (kvh * g + gi) * d:(kvh * g + gi + 1) * d] for gi in range(g)], axis=0)
        sk = jnp.concatenate(
            [jnp.broadcast_to(sink[:, kvh * g + gi:kvh * g + gi + 1], (tq, 1)) for gi in range(g)],
            axis=0)
        s = _dot_nt(qs, k_all[:, kvh * d:(kvh + 1) * d])
        if bias is not None:
            s = s + bias
        m = jnp.maximum(jnp.max(s, axis=1, keepdims=True), sk)
        p = jnp.exp2(s - m)
        den = jnp.sum(p, axis=1, keepdims=True) + jnp.exp2(sk - m)
        o = _dot(p.astype(BF16), v_all[:, kvh * d:(kvh + 1) * d]) / den
        outs += [o[gi * tq:(gi + 1) * tq] for gi in range(g)]
    o_ref[0] = jnp.concatenate(outs, axis=1).astype(o_ref.dtype)


def _swa_attention(qk_q, p_q, qk_c, p_c, sink, band, tq):
    b, nq, _ = qk_q.shape
    nx = qk_c.shape[1]
    w = SWA_WINDOW
    qcol = P_OFF["sq"] // 512
    kcol = P_OFF["sk"] // LANES
    vcol = P_OFF["sv"] // LANES
    sink_p = jnp.zeros((1, LANES), F32).at[0, :SWA_HEADS].set(sink)
    in_specs = [pl.BlockSpec((1, tq, 512), lambda bi, i: (bi, i, qcol))]
    args = [qk_q]
    if band:
        r = tq // w
        last = nq // w - 1
        prev = lambda i: jnp.maximum(i * r - 1, 0)
        nxt = lambda i: jnp.minimum((i + 1) * r, last)
        for arr, colb in ((qk_q, kcol), (p_q, vcol)):
            in_specs += [
                pl.BlockSpec((1, w, LANES), lambda bi, i, colb=colb: (bi, prev(i), colb)),
                pl.BlockSpec((1, tq, LANES), lambda bi, i, colb=colb: (bi, i, colb)),
                pl.BlockSpec((1, w, LANES), lambda bi, i, colb=colb: (bi, nxt(i), colb)),
            ]
            args += [arr, arr, arr]
    in_specs += [
        pl.BlockSpec((1, nx, LANES), lambda bi, i: (bi, 0, kcol)),
        pl.BlockSpec((1, nx, LANES), lambda bi, i: (bi, 0, vcol)),
        pl.BlockSpec((1, LANES), lambda bi, i: (0, 0)),
    ]
    args += [qk_c, p_c, sink_p]
    return pl.pallas_call(
        functools.partial(_swa_kernel, band=band, n_lat=nq),
        out_shape=jax.ShapeDtypeStruct((b, nq, SWA_HEADS * SWA_D), BF16),
        grid=(b, nq // tq),
        in_specs=in_specs,
        out_specs=pl.BlockSpec((1, tq, SWA_HEADS * SWA_D), lambda bi, i: (bi, i, 0)),
        compiler_params=_cparams(("arbitrary", "arbitrary")),
        name="swa_attention",
    )(*args)


def _merge_kernel(x_ref, ya_ref, yb_ref, yc_ref, yd_ref, g0_ref, g1_ref, g2_ref, g3_ref,
                  wb_ref, wo_ref, gate_ref, o_ref):
    acc = None
    for y_ref, g_ref, i in ((ya_ref, g0_ref, 0), (yb_ref, g1_ref, 1), (yc_ref, g2_ref, 2),
                            (yd_ref, g3_ref, 3)):
        term = jax.nn.sigmoid(g_ref[0].astype(F32)) * _dot(y_ref[0], wb_ref[i])
        acc = term if acc is None else acc + term
    out = _dot(acc.astype(BF16), wo_ref[...])
    o_ref[0] = x_ref[0] + gate_ref[0] * out


def _merge(x, ys, p, mod, w_branch, w_out, tm):
    b, n, d = x.shape
    bm = mod.shape[0]
    mod_b = (lambda bi: bi) if bm == b else (lambda bi: 0)
    gcol = P_OFF["gates"] // d
    y_spec = pl.BlockSpec((1, tm, BRANCH_WIDTH), lambda bi, i: (bi, i, 0))
    gate_specs = [pl.BlockSpec((1, tm, d), lambda bi, i, k=k: (bi, i, gcol + k))
                  for k in range(N_BRANCHES)]
    const = pl.Buffered(1)
    return pl.pallas_call(
        _merge_kernel,
        out_shape=jax.ShapeDtypeStruct((b, n, d), F32),
        grid=(b, n // tm),
        in_specs=[pl.BlockSpec((1, tm, d), lambda bi, i: (bi, i, 0)),
                  y_spec, y_spec, y_spec, y_spec, *gate_specs,
                  pl.BlockSpec((N_BRANCHES, BRANCH_WIDTH, d), lambda bi, i: (0, 0, 0),
                               pipeline_mode=const),
                  pl.BlockSpec((d, d), lambda bi, i: (0, 0), pipeline_mode=const),
                  pl.BlockSpec((1, 1, d), lambda bi, i: (mod_b(bi), 0, 2))],
        out_specs=pl.BlockSpec((1, tm, d), lambda bi, i: (bi, i, 0)),
        compiler_params=_cparams(("arbitrary", "arbitrary")),
        name="merge_branches",
    )(x, *ys, p, p, p, p, w_branch, w_out, mod)


def _ffn_kernel(x_ref, g_ref, sh_ref, sc_ref, gate_ref, wg_ref, wu_ref, wd_ref, fg_ref, o_ref,
                h_scr, acc_scr, *, final):
    k = pl.program_id(2)

    @pl.when(k == 0)
    def _():
        h_scr[...] = _norm_mod(x_ref[0], g_ref[...], sh_ref[0], sc_ref[0]).astype(BF16)
        acc_scr[...] = jnp.zeros(acc_scr.shape, F32)

    h = h_scr[...]
    gate = _dot(h, wg_ref[...])
    up = _dot(h, wu_ref[...])
    act = (gate * jax.nn.sigmoid(gate)) * up
    acc_scr[...] += _dot(act.astype(BF16), wd_ref[...])

    @pl.when(k == pl.num_programs(2) - 1)
    def _():
        y = x_ref[0] + gate_ref[0] * acc_scr[...]
        if final:
            y = y * lax.rsqrt(jnp.mean(y * y, axis=-1, keepdims=True) + NORM_EPS) * fg_ref[...]
        o_ref[0] = y


def _ffn(x, g, mod, w_up, w_down, final_g, final, tm, th=512):
    b, n, d = x.shape
    bm = mod.shape[0]
    mod_b = (lambda bi: bi) if bm == b else (lambda bi: 0)
    hidden = w_down.shape[0]
    nh = hidden // th
    return pl.pallas_call(
        functools.partial(_ffn_kernel, final=final),
        out_shape=jax.ShapeDtypeStruct((b, n, d), F32),
        grid=(b, n // tm, nh),
        in_specs=[
            pl.BlockSpec((1, tm, d), lambda bi, i, k: (bi, i, 0)),
            pl.BlockSpec((1, d), lambda bi, i, k: (0, 0)),
            pl.BlockSpec((1, 1, d), lambda bi, i, k: (mod_b(bi), 0, 3)),
            pl.BlockSpec((1, 1, d), lambda bi, i, k: (mod_b(bi), 0, 4)),
            pl.BlockSpec((1, 1, d), lambda bi, i, k: (mod_b(bi), 0, 5)),
            pl.BlockSpec((d, th), lambda bi, i, k: (0, k)),
            pl.BlockSpec((d, th), lambda bi, i, k: (0, nh + k)),
            pl.BlockSpec((th, d), lambda bi, i, k: (k, 0)),
            pl.BlockSpec((1, d), lambda bi, i, k: (0, 0)),
        ],
        out_specs=pl.BlockSpec((1, tm, d), lambda bi, i, k: (bi, i, 0)),
        scratch_shapes=[pltpu.VMEM((tm, d), BF16), pltpu.VMEM((tm, d), F32)],
        compiler_params=_cparams(("arbitrary", "arbitrary", "arbitrary")),
        name="ffn",
    )(x, g.reshape(1, d), mod, mod, mod, w_up, w_up, w_down, final_g.reshape(1, d))


def _arrange_w_in(w_in, gate_b):
    cols = [w_in[:, _SRC[name][0]:_SRC[name][0] + _SRC[name][1]] for name in _P_ORDER]
    w_p = jnp.concatenate(cols, axis=1).astype(BF16)
    d = w_in.shape[0]
    mg0 = _SRC["mg"][0]
    hh = MLSTM_HEADS
    wg = w_in[:, mg0:mg0 + 4 * hh]
    w_mg = jnp.zeros((d, 2 * LANES), F32)
    w_mg = w_mg.at[:, 0:hh].set(wg[:, 0:hh]).at[:, hh:2 * hh].set(wg[:, 2 * hh:3 * hh])
    w_mg = w_mg.at[:, LANES:LANES + hh].set(wg[:, hh:2 * hh])
    w_mg = w_mg.at[:, LANES + hh:LANES + 2 * hh].set(wg[:, 3 * hh:4 * hh])
    gb = jnp.zeros((1, 2 * LANES), F32)
    gb = gb.at[0, 0:hh].set(gate_b[0]).at[0, hh:2 * hh].set(gate_b[2])
    gb = gb.at[0, LANES:LANES + hh].set(gate_b[1]).at[0, LANES + hh:LANES + 2 * hh].set(gate_b[3])
    return w_p, w_mg.astype(BF16), gb


def _pick(n, pref):
    t = min(n, pref)
    while n % t:
        t //= 2
    return t


def kernel(x, c, ctx, c_ctx, ada_w, ada_b, norm1_g, w_in, mlstm_gate_b, mlstm_norm_g, diff_lambda,
           diff_norm_g, swa_sink, gqa_q_norm_g, gqa_k_norm_g, w_branch, w_out, norm2_g, w_up,
           w_down, final_norm_g):
    b, n_lat, d = x.shape
    n_ctx = ctx.shape[1]
    depth = ada_w.shape[0]

    cond = jnp.zeros((8, d), F32).at[0:b].set(c).at[b].set(c_ctx)
    mod = _ada_modulation(cond, ada_w, ada_b)

    t64_l = _rope_tables(n_lat, DIFF_D, identity=False)
    t128_l = _rope_tables(n_lat, GQA_D, identity=False)
    t64_c = _rope_tables(n_ctx, DIFF_D, identity=True)
    t128_c = _rope_tables(n_ctx, GQA_D, identity=True)

    tm_proj = _pick(n_lat, 1024)
    tm_prep = _pick(n_lat, 512)
    tq_dense = _pick(n_lat, 1024)
    tk_dense = _pick(n_lat, 1024)
    tq_swa = _pick(n_lat, 256)
    tm_merge = _pick(n_lat, 256)
    tm_ffn = _pick(n_lat, 512)

    xc = ctx
    for l in range(depth):
        last = l == depth - 1
        lam_init = 0.8 - 0.6 * math.exp(-0.3 * l)
        mod_l = mod[l, 0:b].reshape(b, 1, 6 * d)
        mod_c = mod[l, b:b + 1].reshape(1, 1, 6 * d)
        w_p, w_mg, gate_b = _arrange_w_in(w_in[l], mlstm_gate_b[l])
        wb = w_branch[l].astype(BF16)
        wo = w_out[l].astype(BF16)
        wu = w_up[l].astype(BF16)
        wd = w_down[l].astype(BF16)

        p_l, mg_l = _input_projection(x, norm1_g[l], mod_l, w_p, w_mg, tm_proj)
        p_c, mg_c = _input_projection(xc, norm1_g[l], mod_c, w_p, w_mg, n_ctx)
        qk_l = _prep_qk(p_l, t64_l, t128_l, gqa_q_norm_g[l], gqa_k_norm_g[l], tm_prep)
        qk_c = _prep_qk(p_c, t64_c, t128_c, gqa_q_norm_g[l], gqa_k_norm_g[l], n_ctx)

        ya_c, ya_l = _mlstm_mixer(p_c, mg_c, p_l, mg_l, gate_b, mlstm_norm_g[l], not last)
        yb_l = _diff_attention(qk_l, qk_l, p_l, qk_c, p_c, diff_lambda[l], diff_norm_g[l],
                               lam_init, tq_dense, tk_dense)
        yc_l = _swa_attention(qk_l, p_l, qk_c, p_c, swa_sink[l], True, tq_swa)
        yd_l = _gqa_attention(qk_l, qk_l, p_l, qk_c, p_c, tq_dense, tk_dense)
        x = _merge(x, (ya_l, yb_l, yc_l, yd_l), p_l, mod_l, wb, wo, tm_merge)
        x = _ffn(x, norm2_g[l], mod_l, wu, wd, final_norm_g, last, tm_ffn)

        if not last:
            yb_c = _diff_attention(qk_c, qk_c, p_c, None, None, diff_lambda[l], diff_norm_g[l],
                                   lam_init, n_ctx, n_ctx)
            yc_c = _swa_attention(qk_c, p_c, qk_c, p_c, swa_sink[l], False, n_ctx)
            yd_c = _gqa_attention(qk_c, qk_c, p_c, None, None, n_ctx, n_ctx)
            xc = _merge(xc, (ya_c, yb_c, yc_c, yd_c), p_c, mod_c, wb, wo, n_ctx)
            xc = _ffn(xc, norm2_g[l], mod_c, wu, wd, final_norm_g, False, n_ctx)
    return x
```

```python
import functools
import math

import jax
import jax.numpy as jnp
from jax import lax
from jax.experimental import pallas as pl
from jax.experimental.pallas import tpu as pltpu

F32 = jnp.float32
BF16 = jnp.bfloat16

D_MODEL = 2048
GRID_W = 64
ROPE_THETA = 10000.0
NORM_EPS = 1e-6
NEG_INF = -1e30
LOG2E = 1.4426950408889634

N_BRANCHES = 4
BRANCH_WIDTH = 512
MLSTM_HEADS, MLSTM_DK, MLSTM_DV, MLSTM_CHUNK = 4, 64, 128, 128
DIFF_HEADS, DIFF_D = 4, 64
SWA_HEADS, SWA_KV_HEADS, SWA_D, SWA_WINDOW = 8, 2, 64, 128
GQA_HEADS, GQA_KV_HEADS, GQA_D = 4, 2, 128
FFN_HIDDEN = 5632

LANES = 128
VMEM_LIMIT_BYTES = 56 * 1024 * 1024

_SRC = {}
_acc = 0
for _name, _w in (("mq", 256), ("mk", 256), ("mv", 512), ("mo", 512), ("mg", 16),
                  ("dq", 512), ("dk", 512), ("dv", 512), ("sq", 512), ("sk", 128), ("sv", 128),
                  ("gq", 512), ("gk", 256), ("gv", 256), ("gates", 8192)):
    _SRC[_name] = (_acc, _w)
    _acc += _w

_P_ORDER = ("dq", "dk", "sq", "sk", "gq", "gk", "sv", "mv", "mo", "mq", "mk", "gates", "dv", "gv")
P_OFF = {}
_acc = 0
for _name in _P_ORDER:
    P_OFF[_name] = _acc
    _acc += _SRC[_name][1]
P_WIDTH = _acc
QK_WIDTH = P_OFF["sv"]
PROJ_TN = 768


def _cparams(sem):
    return pltpu.CompilerParams(dimension_semantics=sem, vmem_limit_bytes=VMEM_LIMIT_BYTES)


def _dot(a, b):
    return jnp.dot(a, b, preferred_element_type=F32)


def _dot_nt(a, b):
    return lax.dot_general(a, b, (((1,), (1,)), ((), ())), preferred_element_type=F32)


def _ada_kernel(s_ref, w_ref, b_ref, o_ref):
    s = s_ref[...]
    s = s * jax.nn.sigmoid(s)
    o_ref[0] = _dot(s.astype(BF16), w_ref[0].astype(BF16)) + b_ref[0]


def _ada_modulation(cond_rows, ada_w, ada_b):
    depth, d, n6 = ada_w.shape
    rows = cond_rows.shape[0]
    tn = 1024
    return pl.pallas_call(
        _ada_kernel,
        out_shape=jax.ShapeDtypeStruct((depth, rows, n6), F32),
        grid=(depth, n6 // tn),
        in_specs=[
            pl.BlockSpec((rows, d), lambda l, j: (0, 0)),
            pl.BlockSpec((1, d, tn), lambda l, j: (l, 0, j)),
            pl.BlockSpec((1, 1, tn), lambda l, j: (l, 0, j)),
        ],
        out_specs=pl.BlockSpec((1, rows, tn), lambda l, j: (l, 0, j)),
        compiler_params=_cparams(("arbitrary", "arbitrary")),
        name="ada_modulation",
    )(cond_rows, ada_w, ada_b.reshape(depth, 1, n6))


def _norm_mod(x, g, shift, scale):
    y = x * lax.rsqrt(jnp.mean(x * x, axis=-1, keepdims=True) + NORM_EPS)
    return (y * g) * (1.0 + scale) + shift


def _proj_kernel(x_ref, g_ref, sh_ref, sc_ref, w_ref, wmg_ref, p_ref, mg_ref, h_scr):
    @pl.when(pl.program_id(2) == 0)
    def _():
        h = _norm_mod(x_ref[0], g_ref[...], sh_ref[0], sc_ref[0]).astype(BF16)
        h_scr[...] = h
        mg_ref[0] = _dot(h, wmg_ref[...])

    p_ref[0] = _dot(h_scr[...], w_ref[...]).astype(BF16)


def _input_projection(x, g, mod, w_p, w_mg, tm):
    b, n, d = x.shape
    bm = mod.shape[0]
    mod_b = (lambda bi: bi) if bm == b else (lambda bi: 0)
    return pl.pallas_call(
        _proj_kernel,
        out_shape=(jax.ShapeDtypeStruct((b, n, P_WIDTH), BF16),
                   jax.ShapeDtypeStruct((b, n, 2 * LANES), F32)),
        grid=(b, n // tm, P_WIDTH // PROJ_TN),
        in_specs=[
            pl.BlockSpec((1, tm, d), lambda bi, i, j: (bi, i, 0)),
            pl.BlockSpec((1, d), lambda bi, i, j: (0, 0)),
            pl.BlockSpec((1, 1, d), lambda bi, i, j: (mod_b(bi), 0, 0)),
            pl.BlockSpec((1, 1, d), lambda bi, i, j: (mod_b(bi), 0, 1)),
            pl.BlockSpec((d, PROJ_TN), lambda bi, i, j: (0, j)),
            pl.BlockSpec((d, 2 * LANES), lambda bi, i, j: (0, 0)),
        ],
        out_specs=(pl.BlockSpec((1, tm, PROJ_TN), lambda bi, i, j: (bi, i, j)),
                   pl.BlockSpec((1, tm, 2 * LANES), lambda bi, i, j: (bi, i, 0))),
        scratch_shapes=[pltpu.VMEM((tm, d), BF16)],
        compiler_params=_cparams(("arbitrary", "arbitrary", "arbitrary")),
        name="input_projection",
    )(x, g.reshape(1, d), mod, mod, w_p, w_mg)


def _rope(x, cos, sin_lo, sin_hi, half):
    width = x.shape[-1]
    return (x * cos + pltpu.roll(x, width - half, 1) * sin_lo + pltpu.roll(x, half, 1) * sin_hi)


def _prep_kernel(p_ref, t64_ref, t128_ref, qg_ref, kg_ref, o_ref):
    x = p_ref[0].astype(F32)
    c64, sl64, sh64 = t64_ref[0], t64_ref[1], t64_ref[2]
    c128, sl128, sh128 = t128_ref[0], t128_ref[1], t128_ref[2]
    qscale64 = LOG2E * DIFF_D ** -0.5
    qscale128 = LOG2E * GQA_D ** -0.5
    for blk in range(QK_WIDTH // LANES):
        lo = blk * LANES
        xb = x[:, lo:lo + LANES]
        if lo < P_OFF["gq"]:
            is_q = (lo < P_OFF["dk"]) or (P_OFF["sq"] <= lo < P_OFF["sk"])
            y = _rope(xb, c64, sl64, sh64, DIFF_D // 4)
            if is_q:
                y = y * qscale64
        else:
            is_q = lo < P_OFF["gk"]
            g = qg_ref[...] if is_q else kg_ref[...]
            y = xb * lax.rsqrt(jnp.mean(xb * xb, axis=-1, keepdims=True) + NORM_EPS) * g
            y = _rope(y, c128, sl128, sh128, GQA_D // 4)
            if is_q:
                y = y * qscale128
        o_ref[0, :, lo:lo + LANES] = y.astype(BF16)


def _prep_qk(p, t64, t128, qg, kg, tm):
    b, n, _ = p.shape
    return pl.pallas_call(
        _prep_kernel,
        out_shape=jax.ShapeDtypeStruct((b, n, QK_WIDTH), BF16),
        grid=(b, n // tm),
        in_specs=[
            pl.BlockSpec((1, tm, QK_WIDTH), lambda bi, i: (bi, i, 0)),
            pl.BlockSpec((3, tm, LANES), lambda bi, i: (0, i, 0)),
            pl.BlockSpec((3, tm, LANES), lambda bi, i: (0, i, 0)),
            pl.BlockSpec((1, LANES), lambda bi, i: (0, 0)),
            pl.BlockSpec((1, LANES), lambda bi, i: (0, 0)),
        ],
        out_specs=pl.BlockSpec((1, tm, QK_WIDTH), lambda bi, i: (bi, i, 0)),
        compiler_params=_cparams(("arbitrary", "arbitrary")),
        name="prep_qk",
    )(p, t64, t128, qg.reshape(1, LANES), kg.reshape(1, LANES))


def _rope_tables(n, head_dim, identity):
    n_freq = head_dim // 4
    lane = jnp.arange(LANES)
    first_half = (lane % (2 * n_freq)) < n_freq
    if identity:
        cos = jnp.ones((n, LANES), F32)
        zero = jnp.zeros((n, LANES), F32)
        return jnp.stack([cos, zero, zero])
    t = jnp.arange(n)
    pos = jnp.stack([(t // GRID_W).astype(F32), (t % GRID_W).astype(F32)], axis=1)
    inv = ROPE_THETA ** (-jnp.arange(n_freq, dtype=F32) / n_freq)
    ang = pos[:, :, None] * inv
    axis = (lane % head_dim) // (2 * n_freq)
    freq = lane % n_freq
    ang_l = ang[:, axis, freq]
    cos, sin = jnp.cos(ang_l), jnp.sin(ang_l)
    sin_lo = jnp.where(first_half, -sin, 0.0)
    sin_hi = jnp.where(first_half, 0.0, sin)
    return jnp.stack([cos, sin_lo, sin_hi])


def _log_sigmoid(x):
    return jnp.minimum(x, 0.0) - jnp.log(1.0 + jnp.exp(-jnp.abs(x)))


def _split_dot(tri, x):
    hi = x.astype(BF16)
    r1 = x - hi.astype(F32)
    mid = r1.astype(BF16)
    lo = (r1 - mid.astype(F32)).astype(BF16)
    return _dot(tri, hi) + _dot(tri, mid) + _dot(tri, lo)


def _mlstm_kernel(*refs, reverse, finish):
    if finish:
        (q_ref, k_ref, v_ref, mg_ref, gb_ref, c0_ref, m0_ref, hf_ref, o_ref, ng_ref,
         out_ref, cfin_ref, mfin_ref, c_scr, m_scr) = refs
    else:
        (q_ref, k_ref, v_ref, mg_ref, gb_ref, c0_ref, m0_ref,
         out_ref, cfin_ref, mfin_ref, c_scr, m_scr) = refs
    L = MLSTM_CHUNK
    j = pl.program_id(1)

    @pl.when(j == 0)
    def _():
        c_scr[...] = c0_ref[0]
        m_scr[...] = m0_ref[0]

    q_all = q_ref[0]
    k_all = k_ref[0]
    v_all = v_ref[0]
    kt_all = k_all.astype(F32).T.astype(BF16)

    gt = mg_ref[0] + gb_ref[...]
    ig = gt[:, :LANES]
    lf = _log_sigmoid(gt[:, LANES:])
    row = lax.broadcasted_iota(jnp.int32, (L, L), 0)
    col = lax.broadcasted_iota(jnp.int32, (L, L), 1)
    mask = (col >= row) if reverse else (col <= row)
    tri = jnp.where(mask, 1.0, 0.0).astype(BF16)
    bc = _split_dot(tri, lf)
    a = ig - bc
    a_t = a.T
    total = bc[0:1, :] if reverse else bc[L - 1:L, :]
    ones_col = jnp.where(col == 0, 1.0, 0.0).astype(BF16)

    dir_base = MLSTM_HEADS if reverse else 0
    houts = []
    for h in range(MLSTM_HEADS):
        c = dir_base + h
        a_row = a_t[c:c + 1, :]
        a_col = a[:, c:c + 1]
        bc_col = bc[:, c:c + 1]
        tot = total[:, c:c + 1]
        m_prev = m_scr[h][:, 0:1]
        dm = jnp.where(mask, jnp.broadcast_to(a_row, (L, L)), NEG_INF)
        cm = jnp.max(dm, axis=1, keepdims=True)
        mt = jnp.maximum(cm, m_prev)
        w_intra = jnp.exp(dm - mt)
        w_inter = jnp.exp(m_prev - mt)
        q = q_all[:, h * MLSTM_DK:(h + 1) * MLSTM_DK]
        k = k_all[:, h * MLSTM_DK:(h + 1) * MLSTM_DK]
        v = v_all[:, h * MLSTM_DV:(h + 1) * MLSTM_DV]
        v_ext = jnp.concatenate([v, ones_col], axis=1)
        kscale = MLSTM_DK ** -0.5
        s = _dot_nt(q, k) * kscale * w_intra
        r = _dot(s.astype(BF16), v_ext)
        qc = _dot(q, c_scr[h].astype(BF16))
        t = w_inter * qc + r
        num = t[:, :MLSTM_DV]
        den = t[:, MLSTM_DV:MLSTM_DV + 1]
        m_t = bc_col + mt
        houts.append(num / jnp.maximum(jnp.abs(den), jnp.exp(-m_t)))
        a_max = jnp.max(a_row, axis=1, keepdims=True)
        m_new = tot + jnp.maximum(m_prev, a_max)
        decay = jnp.exp(tot + m_prev - m_new)
        w_s = jnp.exp(tot + a_col - m_new)
        kt = kt_all[h * MLSTM_DK:(h + 1) * MLSTM_DK, :]
        upd = _dot(kt, (w_s * v_ext.astype(F32)).astype(BF16))
        c_scr[h] = decay * c_scr[h] + kscale * upd
        m_scr[h] = jnp.broadcast_to(m_new, (1, LANES))

    hcat = jnp.concatenate(houts, axis=1)
    if finish:
        hsum = hf_ref[0] + hcat
        ys = []
        for h in range(MLSTM_HEADS):
            hh = hsum[:, h * MLSTM_DV:(h + 1) * MLSTM_DV]
            ys.append(hh * lax.rsqrt(jnp.mean(hh * hh, axis=-1, keepdims=True) + NORM_EPS))
        y = jnp.concatenate(ys, axis=1) * ng_ref[...]
        out_ref[0] = (y * jax.nn.sigmoid(o_ref[0].astype(F32))).astype(out_ref.dtype)
    else:
        out_ref[0] = hcat

    @pl.when(j == pl.num_programs(1) - 1)
    def _():
        cfin_ref[0] = c_scr[...]
        mfin_ref[0] = m_scr[...]


def _mlstm_scan(p, mg, gate_b, c0, m0, hf, norm_g, *, reverse):
    b, n, _ = p.shape
    L = MLSTM_CHUNK
    nc = n // L
    finish = hf is not None
    cidx = (lambda j: nc - 1 - j) if reverse else (lambda j: j)
    w256, w512 = 256, 512
    in_specs = [
        pl.BlockSpec((1, L, w256), lambda bi, j: (bi, cidx(j), P_OFF["mq"] // w256)),
        pl.BlockSpec((1, L, w256), lambda bi, j: (bi, cidx(j), P_OFF["mk"] // w256)),
        pl.BlockSpec((1, L, w512), lambda bi, j: (bi, cidx(j), P_OFF["mv"] // w512)),
        pl.BlockSpec((1, L, 2 * LANES), lambda bi, j: (bi, cidx(j), 0)),
        pl.BlockSpec((1, 2 * LANES), lambda bi, j: (0, 0)),
        pl.BlockSpec((1, MLSTM_HEADS, MLSTM_DK, 2 * LANES), lambda bi, j: (bi, 0, 0, 0)),
        pl.BlockSpec((1, MLSTM_HEADS, 1, LANES), lambda bi, j: (bi, 0, 0, 0)),
    ]
    args = [p, p, p, mg, gate_b, c0, m0]
    if finish:
        in_specs += [
            pl.BlockSpec((1, L, w512), lambda bi, j: (bi, cidx(j), 0)),
            pl.BlockSpec((1, L, w512), lambda bi, j: (bi, cidx(j), P_OFF["mo"] // w512)),
            pl.BlockSpec((1, w512), lambda bi, j: (0, 0)),
        ]
        args += [hf, p, norm_g.reshape(1, w512)]
    return pl.pallas_call(
        functools.partial(_mlstm_kernel, reverse=reverse, finish=finish),
        out_shape=(jax.ShapeDtypeStruct((b, n, w512), BF16 if finish else F32),
                   jax.ShapeDtypeStruct(c0.shape, F32),
                   jax.ShapeDtypeStruct(m0.shape, F32)),
        grid=(b, nc),
        in_specs=in_specs,
        out_specs=(pl.BlockSpec((1, L, w512), lambda bi, j: (bi, cidx(j), 0)),
                   pl.BlockSpec((1, MLSTM_HEADS, MLSTM_DK, 2 * LANES), lambda bi, j: (bi, 0, 0, 0)),
                   pl.BlockSpec((1, MLSTM_HEADS, 1, LANES), lambda bi, j: (bi, 0, 0, 0))),
        scratch_shapes=[pltpu.VMEM((MLSTM_HEADS, MLSTM_DK, 2 * LANES), F32),
                        pltpu.VMEM((MLSTM_HEADS, 1, LANES), F32)],
        compiler_params=_cparams(("arbitrary", "arbitrary")),
        name="mlstm_bwd" if reverse else "mlstm_fwd",
    )(*args)


def _mlstm_mixer(p_c, mg_c, p_l, mg_l, gate_b, norm_g, ctx_out):
    b = p_l.shape[0]
    c0 = jnp.zeros((b, MLSTM_HEADS, MLSTM_DK, 2 * LANES), F32)
    m0 = jnp.zeros((b, MLSTM_HEADS, 1, LANES), F32)
    hc_f, cf, mf = _mlstm_scan(p_c, mg_c, gate_b, c0, m0, None, None, reverse=False)
    hl_f, _, _ = _mlstm_scan(p_l, mg_l, gate_b, cf, mf, None, None, reverse=False)
    yc, cb, mb = _mlstm_scan(p_c, mg_c, gate_b, c0, m0, hc_f, norm_g, reverse=True)
    yl, _, _ = _mlstm_scan(p_l, mg_l, gate_b, cb, mb, hl_f, norm_g, reverse=True)
    return (yc if ctx_out else None), yl


def _ones_lane0(rows):
    col = lax.broadcasted_iota(jnp.int32, (rows, LANES), 1)
    return jnp.where(col == 0, 1.0, 0.0).astype(BF16)


def _flash_kernel(*refs, mode, k0, n_rest, tkc, lam_init):
    if mode == "diff":
        (q_ref, k_ref, v_ref, lam_ref, g_ref, o_ref, s0, s1, p0, p1, a0, a1, m_scr, acc_scr) = refs
        d = DIFF_D
        q = q_ref[0]
        qs = [q[:, 0:d], q[:, d:2 * d]]
        ksl = [slice(0, d), slice(d, 2 * d)]
    else:
        (qa_ref, qb_ref, k_ref, v_ref, o_ref, s0, s1, p0, p1, a0, a1, m_scr, acc_scr) = refs
        qs = [qa_ref[0], qb_ref[0]]
        ksl = [slice(None), slice(None)]
    s_bufs, p_bufs, a_bufs = (s0, s1), (p0, p1), (a0, a1)
    align = math.gcd(k0, tkc)

    def rows(c):
        if isinstance(c, int):
            return (0, k0) if c == 0 else (k0 + (c - 1) * tkc, tkc)
        return pl.multiple_of(k0 + (c - 1) * tkc, align), tkc

    def qk_stage(c, par):
        start, size = rows(c)
        kc = k_ref[0, pl.ds(start, size), :]
        for st in range(2):
            s_bufs[par][st, :, 0:size] = _dot_nt(qs[st], kc[:, ksl[st]])

    def softmax_stage(size, par):
        for st in range(2):
            s = s_bufs[par][st, :, 0:size]
            m_prev = m_scr[st]
            m_new = jnp.maximum(m_prev, jnp.max(s, axis=1, keepdims=True))
            a_bufs[par][st] = jnp.exp2(m_prev - m_new)
            p_bufs[par][st, :, 0:size] = jnp.exp2(s - m_new[:, 0:1]).astype(BF16)
            m_scr[st] = m_new

    def pv_stage(c, par):
        start, size = rows(c)
        vc = v_ref[0, pl.ds(start, size), :]
        v_ext = jnp.concatenate([vc, _ones_lane0(size)], axis=1)
        for st in range(2):
            a = a_bufs[par][st]
            acc_scr[st] = (jnp.concatenate([a, a], axis=1) * acc_scr[st]
                           + _dot(p_bufs[par][st, :, 0:size], v_ext))

    m_scr[...] = jnp.full(m_scr.shape, NEG_INF, F32)
    acc_scr[...] = jnp.zeros(acc_scr.shape, F32)

    def slot(c, par, first=False, last=False):
        if not last:
            qk_stage(c + 1, 1 - par)
        softmax_stage(rows(c)[1], par)
        if not first:
            pv_stage(c - 1, 1 - par)

    n = 1 + n_rest
    qk_stage(0, 0)
    slot(0, 0, first=True, last=(n == 1))
    if n > 1:
        static_head = min(2, n - 1)
        for c in range(1, static_head):
            slot(c, c % 2)
        n_pairs = max(n - 1 - static_head, 0) // 2
        done = static_head
        if n_pairs >= 2:
            def body(i, carry):
                c = static_head + 2 * i
                slot(c, static_head % 2)
                slot(c + 1, 1 - static_head % 2)
                return carry
            lax.fori_loop(0, n_pairs, body, 0)
            done = static_head + 2 * n_pairs
        for c in range(done, n - 1):
            slot(c, c % 2)
        slot(n - 1, (n - 1) % 2, last=True)
    pv_stage(n - 1, (n - 1) % 2)

    acc_a, acc_b = acc_scr[0], acc_scr[1]
    o_a = acc_a[:, :LANES] / acc_a[:, LANES:LANES + 1]
    o_b = acc_b[:, :LANES] / acc_b[:, LANES:LANES + 1]
    if mode == "diff":
        lp = lam_ref[0]
        lam = (jnp.exp(jnp.sum(lp[0:1] * lp[1:2], axis=1, keepdims=True))
               - jnp.exp(jnp.sum(lp[2:3] * lp[3:4], axis=1, keepdims=True)) + lam_init)
        o = o_a - lam * o_b
        y = o * lax.rsqrt(jnp.mean(o * o, axis=-1, keepdims=True) + NORM_EPS) * g_ref[...]
        o_ref[0] = (y * (1.0 - lam_init)).astype(o_ref.dtype)
    else:
        o_ref[0] = jnp.concatenate([o_a, o_b], axis=1).astype(o_ref.dtype)


def _flash_attention(mode, q_arr, qcol, k_arr, v_arr, k0, tq, tkc, lam_params=None, norm_g=None,
                     lam_init=0.0):
    b, nq, _ = q_arr.shape
    nk = k_arr.shape[1]
    heads = k_arr.shape[2] // LANES
    wbuf = max(k0, tkc)
    kv_spec = pl.BlockSpec((1, nk, LANES), lambda bi, h, i: (bi, 0, h),
                           pipeline_mode=pl.Buffered(1))
    if mode == "diff":
        in_specs = [pl.BlockSpec((1, tq, LANES), lambda bi, h, i: (bi, i, qcol + h)),
                    kv_spec, kv_spec,
                    pl.BlockSpec((1, 4, DIFF_D), lambda bi, h, i: (h, 0, 0)),
                    pl.BlockSpec((1, LANES), lambda bi, h, i: (0, h))]
        args = [q_arr, k_arr, v_arr, jnp.transpose(lam_params, (1, 0, 2)),
                norm_g.reshape(1, heads * LANES)]
        out_w, out_spec = heads * LANES, pl.BlockSpec((1, tq, LANES), lambda bi, h, i: (bi, i, h))
    else:
        in_specs = [pl.BlockSpec((1, tq, LANES), lambda bi, h, i: (bi, i, qcol + 2 * h)),
                    pl.BlockSpec((1, tq, LANES), lambda bi, h, i: (bi, i, qcol + 2 * h + 1)),
                    kv_spec, kv_spec]
        args = [q_arr, q_arr, k_arr, v_arr]
        out_w = 2 * heads * LANES
        out_spec = pl.BlockSpec((1, tq, 2 * LANES), lambda bi, h, i: (bi, i, h))
    return pl.pallas_call(
        functools.partial(_flash_kernel, mode=mode, k0=k0, n_rest=(nk - k0) // tkc, tkc=tkc,
                          lam_init=lam_init),
        out_shape=jax.ShapeDtypeStruct((b, nq, out_w), BF16),
        grid=(b, heads, nq // tq),
        in_specs=in_specs,
        out_specs=out_spec,
        scratch_shapes=[pltpu.VMEM((2, tq, wbuf), F32), pltpu.VMEM((2, tq, wbuf), F32),
                        pltpu.VMEM((2, tq, wbuf), BF16), pltpu.VMEM((2, tq, wbuf), BF16),
                        pltpu.VMEM((2, tq, LANES), F32), pltpu.VMEM((2, tq, LANES), F32),
                        pltpu.VMEM((2, tq, LANES), F32), pltpu.VMEM((2, tq, 2 * LANES), F32)],
        compiler_params=_cparams(("arbitrary",) * 3),
        name=mode + "_flash",
    )(*args)


def _swa_kernel(*refs, band, n_lat):
    if band:
        (q_ref, kp_ref, kc_ref, kn_ref, vp_ref, vc_ref, vn_ref, kx_ref, vx_ref, sink_ref,
         o_ref) = refs
    else:
        (q_ref, kx_ref, vx_ref, sink_ref, o_ref) = refs
    tq = q_ref.shape[1]
    d = SWA_D
    g = SWA_HEADS // SWA_KV_HEADS
    q = q_ref[0]
    kx = kx_ref[0]
    vx = vx_ref[0]
    nx = kx.shape[0]
    if band:
        w = SWA_WINDOW
        i = pl.program_id(1)
        k_all = jnp.concatenate([kx, kp_ref[0], kc_ref[0], kn_ref[0]], axis=0)
        v_all = jnp.concatenate([vx, vp_ref[0], vc_ref[0], vn_ref[0]], axis=0)
        nkeys = nx + tq + 2 * w
        qpos = i * tq + lax.broadcasted_iota(jnp.int32, (tq, nkeys), 0)
        kidx = lax.broadcasted_iota(jnp.int32, (tq, nkeys), 1)
        kpos = i * tq - w + (kidx - nx)
        valid = (kidx < nx) | ((jnp.abs(qpos - kpos) <= w) & (kpos >= 0) & (kpos < n_lat))
        bias = jnp.where(valid, 0.0, NEG_INF)
        bias = jnp.concatenate([bias] * g, axis=0)
    else:
        k_all, v_all = kx, vx
        bias = None
    sink = sink_ref[...] * LOG2E
    outs = []
    for kvh in range(SWA_KV_HEADS):
        qs = jnp.concatenate(
            [q[:, (kvh * g + gi) * d:(kvh * g + gi + 1) * d] for gi in range(g)], axis=0)
        sk = jnp.concatenate(
            [jnp.broadcast_to(sink[:, kvh * g + gi:kvh * g + gi + 1], (tq, 1)) for gi in range(g)],
            axis=0)
        s = _dot_nt(qs, k_all[:, kvh * d:(kvh + 1) * d])
        if bias is not None:
            s = s + bias
        m = jnp.maximum(jnp.max(s, axis=1, keepdims=True), sk)
        p = jnp.exp2(s - m)
        den = jnp.sum(p, axis=1, keepdims=True) + jnp.exp2(sk - m)
        o = _dot(p.astype(BF16), v_all[:, kvh * d:(kvh + 1) * d]) / den
        outs += [o[gi * tq:(gi + 1) * tq] for gi in range(g)]
    o_ref[0] = jnp.concatenate(outs, axis=1).astype(o_ref.dtype)


def _swa_attention(qk_q, p_q, qk_c, p_c, sink, band, tq):
    b, nq, _ = qk_q.shape
    nx = qk_c.shape[1]
    w = SWA_WINDOW
    qcol = P_OFF["sq"] // 512
    kcol = P_OFF["sk"] // LANES
    vcol = P_OFF["sv"] // LANES
    sink_p = jnp.zeros((1, LANES), F32).at[0, :SWA_HEADS].set(sink)
    in_specs = [pl.BlockSpec((1, tq, 512), lambda bi, i: (bi, i, qcol))]
    args = [qk_q]
    if band:
        r = tq // w
        last = nq // w - 1
        prev = lambda i: jnp.maximum(i * r - 1, 0)
        nxt = lambda i: jnp.minimum((i + 1) * r, last)
        for arr, colb in ((qk_q, kcol), (p_q, vcol)):
            in_specs += [
                pl.BlockSpec((1, w, LANES), lambda bi, i, colb=colb: (bi, prev(i), colb)),
                pl.BlockSpec((1, tq, LANES), lambda bi, i, colb=colb: (bi, i, colb)),
                pl.BlockSpec((1, w, LANES), lambda bi, i, colb=colb: (bi, nxt(i), colb)),
            ]
            args += [arr, arr, arr]
    in_specs += [
        pl.BlockSpec((1, nx, LANES), lambda bi, i: (bi, 0, kcol)),
        pl.BlockSpec((1, nx, LANES), lambda bi, i: (bi, 0, vcol)),
        pl.BlockSpec((1, LANES), lambda bi, i: (0, 0)),
    ]
    args += [qk_c, p_c, sink_p]
    return pl.pallas_call(
        functools.partial(_swa_kernel, band=band, n_lat=nq),
        out_shape=jax.ShapeDtypeStruct((b, nq, SWA_HEADS * SWA_D), BF16),
        grid=(b, nq // tq),
        in_specs=in_specs,
        out_specs=pl.BlockSpec((1, tq, SWA_HEADS * SWA_D), lambda bi, i: (bi, i, 0)),
        compiler_params=_cparams(("arbitrary", "arbitrary")),
        name="swa_attention",
    )(*args)


def _merge_kernel(x_ref, ya_ref, yb_ref, yc_ref, yd_ref, g0_ref, g1_ref, g2_ref, g3_ref,
                  wb_ref, wo_ref, gate_ref, o_ref):
    acc = None
    for y_ref, g_ref, i in ((ya_ref, g0_ref, 0), (yb_ref, g1_ref, 1), (yc_ref, g2_ref, 2),
                            (yd_ref, g3_ref, 3)):
        term = jax.nn.sigmoid(g_ref[0].astype(F32)) * _dot(y_ref[0], wb_ref[i])
        acc = term if acc is None else acc + term
    out = _dot(acc.astype(BF16), wo_ref[...])
    o_ref[0] = x_ref[0] + gate_ref[0] * out


def _merge(x, ys, p, mod, w_branch, w_out, tm):
    b, n, d = x.shape
    bm = mod.shape[0]
    mod_b = (lambda bi: bi) if bm == b else (lambda bi: 0)
    gcol = P_OFF["gates"] // d
    y_spec = pl.BlockSpec((1, tm, BRANCH_WIDTH), lambda bi, i: (bi, i, 0))
    gate_specs = [pl.BlockSpec((1, tm, d), lambda bi, i, k=k: (bi, i, gcol + k))
                  for k in range(N_BRANCHES)]
    const = pl.Buffered(1)
    return pl.pallas_call(
        _merge_kernel,
        out_shape=jax.ShapeDtypeStruct((b, n, d), F32),
        grid=(b, n // tm),
        in_specs=[pl.BlockSpec((1, tm, d), lambda bi, i: (bi, i, 0)),
                  y_spec, y_spec, y_spec, y_spec, *gate_specs,
                  pl.BlockSpec((N_BRANCHES, BRANCH_WIDTH, d), lambda bi, i: (0, 0, 0),
                               pipeline_mode=const),
                  pl.BlockSpec((d, d), lambda bi, i: (0, 0), pipeline_mode=const),
                  pl.BlockSpec((1, 1, d), lambda bi, i: (mod_b(bi), 0, 2))],
        out_specs=pl.BlockSpec((1, tm, d), lambda bi, i: (bi, i, 0)),
        compiler_params=_cparams(("arbitrary", "arbitrary")),
        name="merge_branches",
    )(x, *ys, p, p, p, p, w_branch, w_out, mod)


def _ffn_kernel(x_ref, g_ref, sh_ref, sc_ref, gate_ref, wg_ref, wu_ref, wd_ref, fg_ref, o_ref,
                h_scr, acc_scr, *, final):
    k = pl.program_id(2)

    @pl.when(k == 0)
    def _():
        h_scr[...] = _norm_mod(x_ref[0], g_ref[...], sh_ref[0], sc_ref[0]).astype(BF16)
        acc_scr[...] = jnp.zeros(acc_scr.shape, F32)

    h = h_scr[...]
    gate = _dot(h, wg_ref[...])
    up = _dot(h, wu_ref[...])
    act = (gate * jax.nn.sigmoid(gate)) * up
    acc_scr[...] += _dot(act.astype(BF16), wd_ref[...])

    @pl.when(k == pl.num_programs(2) - 1)
    def _():
        y = x_ref[0] + gate_ref[0] * acc_scr[...]
        if final:
            y = y * lax.rsqrt(jnp.mean(y * y, axis=-1, keepdims=True) + NORM_EPS) * fg_ref[...]
        o_ref[0] = y


def _ffn(x, g, mod, w_up, w_down, final_g, final, tm, th=512):
    b, n, d = x.shape
    bm = mod.shape[0]
    mod_b = (lambda bi: bi) if bm == b else (lambda bi: 0)
    hidden = w_down.shape[0]
    nh = hidden // th
    return pl.pallas_call(
        functools.partial(_ffn_kernel, final=final),
        out_shape=jax.ShapeDtypeStruct((b, n, d), F32),
        grid=(b, n // tm, nh),
        in_specs=[
            pl.BlockSpec((1, tm, d), lambda bi, i, k: (bi, i, 0)),
            pl.BlockSpec((1, d), lambda bi, i, k: (0, 0)),
            pl.BlockSpec((1, 1, d), lambda bi, i, k: (mod_b(bi), 0, 3)),
            pl.BlockSpec((1, 1, d), lambda bi, i, k: (mod_b(bi), 0, 4)),
            pl.BlockSpec((1, 1, d), lambda bi, i, k: (mod_b(bi), 0, 5)),
            pl.BlockSpec((d, th), lambda bi, i, k: (0, k)),
            pl.BlockSpec((d, th), lambda bi, i, k: (0, nh + k)),
            pl.BlockSpec((th, d), lambda bi, i, k: (k, 0)),
            pl.BlockSpec((1, d), lambda bi, i, k: (0, 0)),
        ],
        out_specs=pl.BlockSpec((1, tm, d), lambda bi, i, k: (bi, i, 0)),
        scratch_shapes=[pltpu.VMEM((tm, d), BF16), pltpu.VMEM((tm, d), F32)],
        compiler_params=_cparams(("arbitrary", "arbitrary", "arbitrary")),
        name="ffn",
    )(x, g.reshape(1, d), mod, mod, mod, w_up, w_up, w_down, final_g.reshape(1, d))


def _arrange_w_in(w_in, gate_b):
    cols = [w_in[:, _SRC[name][0]:_SRC[name][0] + _SRC[name][1]] for name in _P_ORDER]
    w_p = jnp.concatenate(cols, axis=1).astype(BF16)
    d = w_in.shape[0]
    mg0 = _SRC["mg"][0]
    hh = MLSTM_HEADS
    wg = w_in[:, mg0:mg0 + 4 * hh]
    w_mg = jnp.zeros((d, 2 * LANES), F32)
    w_mg = w_mg.at[:, 0:hh].set(wg[:, 0:hh]).at[:, hh:2 * hh].set(wg[:, 2 * hh:3 * hh])
    w_mg = w_mg.at[:, LANES:LANES + hh].set(wg[:, hh:2 * hh])
    w_mg = w_mg.at[:, LANES + hh:LANES + 2 * hh].set(wg[:, 3 * hh:4 * hh])
    gb = jnp.zeros((1, 2 * LANES), F32)
    gb = gb.at[0, 0:hh].set(gate_b[0]).at[0, hh:2 * hh].set(gate_b[2])
    gb = gb.at[0, LANES:LANES + hh].set(gate_b[1]).at[0, LANES + hh:LANES + 2 * hh].set(gate_b[3])
    return w_p, w_mg.astype(BF16), gb


def _pick(n, pref):
    t = min(n, pref)
    while n % t:
        t //= 2
    return t


def _tiles(n_lat):
    return {
        "proj": _pick(n_lat, 1024),
        "prep": _pick(n_lat, 512),
        "flash_q": _pick(n_lat, 512),
        "flash_k": _pick(n_lat, 2048),
        "swa": _pick(n_lat, 256),
        "merge": _pick(n_lat, 256),
        "ffn": _pick(n_lat, 512),
    }


def kernel(x, c, ctx, c_ctx, ada_w, ada_b, norm1_g, w_in, mlstm_gate_b, mlstm_norm_g, diff_lambda,
           diff_norm_g, swa_sink, gqa_q_norm_g, gqa_k_norm_g, w_branch, w_out, norm2_g, w_up,
           w_down, final_norm_g):
    b, n_lat, d = x.shape
    n_ctx = ctx.shape[1]
    depth = ada_w.shape[0]

    cond = jnp.zeros((8, d), F32).at[0:b].set(c).at[b].set(c_ctx)
    mod = _ada_modulation(cond, ada_w, ada_b)

    t64_l = _rope_tables(n_lat, DIFF_D, identity=False)
    t128_l = _rope_tables(n_lat, GQA_D, identity=False)
    t64_c = _rope_tables(n_ctx, DIFF_D, identity=True)
    t128_c = _rope_tables(n_ctx, GQA_D, identity=True)

    t = _tiles(n_lat)

    xc = ctx
    for l in range(depth):
        last = l == depth - 1
        lam_init = 0.8 - 0.6 * math.exp(-0.3 * l)
        mod_l = mod[l, 0:b].reshape(b, 1, 6 * d)
        mod_c = mod[l, b:b + 1].reshape(1, 1, 6 * d)
        w_p, w_mg, gate_b = _arrange_w_in(w_in[l], mlstm_gate_b[l])
        wb = w_branch[l].astype(BF16)
        wo = w_out[l].astype(BF16)
        wu = w_up[l].astype(BF16)
        wd = w_down[l].astype(BF16)

        p_l, mg_l = _input_projection(x, norm1_g[l], mod_l, w_p, w_mg, t["proj"])
        p_c, mg_c = _input_projection(xc, norm1_g[l], mod_c, w_p, w_mg, n_ctx)
        qk_l = _prep_qk(p_l, t64_l, t128_l, gqa_q_norm_g[l], gqa_k_norm_g[l], t["prep"])
        qk_c = _prep_qk(p_c, t64_c, t128_c, gqa_q_norm_g[l], gqa_k_norm_g[l], n_ctx)

        def cols(arr, name):
            return arr[:, :, P_OFF[name]:P_OFF[name] + _SRC[name][1]]

        kd_c, vd_c, kg_c, vg_c = cols(qk_c, "dk"), cols(p_c, "dv"), cols(qk_c, "gk"), cols(p_c, "gv")
        kd = jnp.concatenate([kd_c, cols(qk_l, "dk")], axis=1)
        vd = jnp.concatenate([vd_c, cols(p_l, "dv")], axis=1)
        kg = jnp.concatenate([kg_c, cols(qk_l, "gk")], axis=1)
        vg = jnp.concatenate([vg_c, cols(p_l, "gv")], axis=1)
        gq_col = P_OFF["gq"] // LANES

        ya_c, ya_l = _mlstm_mixer(p_c, mg_c, p_l, mg_l, gate_b, mlstm_norm_g[l], not last)
        yb_l = _flash_attention("diff", qk_l, 0, kd, vd, n_ctx, t["flash_q"], t["flash_k"],
                                diff_lambda[l], diff_norm_g[l], lam_init)
        yc_l = _swa_attention(qk_l, p_l, qk_c, p_c, swa_sink[l], True, t["swa"])
        yd_l = _flash_attention("gqa", qk_l, gq_col, kg, vg, n_ctx, t["flash_q"], t["flash_k"])
        x = _merge(x, (ya_l, yb_l, yc_l, yd_l), p_l, mod_l, wb, wo, t["merge"])
        x = _ffn(x, norm2_g[l], mod_l, wu, wd, final_norm_g, last, t["ffn"])

        if not last:
            yb_c = _flash_attention("diff", qk_c, 0, kd_c, vd_c, n_ctx, n_ctx, n_ctx,
                                    diff_lambda[l], diff_norm_g[l], lam_init)
            yc_c = _swa_attention(qk_c, p_c, qk_c, p_c, swa_sink[l], False, n_ctx)
            yd_c = _flash_attention("gqa", qk_c, gq_col, kg_c, vg_c, n_ctx, n_ctx, n_ctx)
            xc = _merge(xc, (ya_c, yb_c, yc_c, yd_c), p_c, mod_c, wb, wo, n_ctx)
            xc = _ffn(xc, norm2_g[l], mod_c, wu, wd, final_norm_g, False, n_ctx)
    return x
```
